```python
import math
import jax
import jax.numpy as jnp
from jax import lax
import numpy as np

D_MODEL = 4096
BATCH = 4
SEQ = 2048
DEPTH = 4
DEC_BATCH = 32
DEC_SEQ = 1
PAST_LEN = 8192
PAGE_SIZE = 128

HD = 128
A_HEADS = 8
A_KVH = 4
A_G = A_HEADS // A_KVH
B_HEADS = 16
B_KVH = 4
B_G = B_HEADS // B_KVH
C_HEADS = 64
C_KVH = 8
C_G = C_HEADS // C_KVH
C_HD = 64
WINDOW = 128
M_HEADS = 4
N_MEM = 256
NUM_BUCKETS = 32
REL_MAX_DIST = 128
QB = 128
D_FF = -(-8 * D_MODEL // (3 * 256)) * 256
N_EVEN = (DEPTH + 1) // 2
N_ODD = DEPTH // 2
A_Q_W = A_HEADS * 2 * HD
A_KV_W = A_KVH * 2 * HD
B_Q_W = B_HEADS * HD
B_KV_W = B_KVH * HD
EVEN_IN = A_Q_W + 2 * A_KV_W + B_Q_W + 2 * B_KV_W + B_HEADS
A_OUT_W = A_HEADS * 2 * HD
B_OUT_W = B_HEADS * HD
EVEN_OUT = A_OUT_W + B_OUT_W
C_Q_W = C_HEADS * C_HD
C_KV_W = C_KVH * C_HD
ODD_IN = C_Q_W + 2 * C_KV_W
C_OUT_W = C_HEADS * C_HD
M_W = M_HEADS * HD
NEG = -1e30
EPS = 1e-6

kernel_name = 'hybrid_diff_fox_swa_decoder_step'


def rms_norm(x, g):
    xf = x.astype(jnp.float32)
    y = xf * lax.rsqrt(jnp.mean(xf * xf, axis=-1, keepdims=True) + EPS)
    return (y * g.astype(jnp.float32)).astype(x.dtype)


def rel_bucket(n):
    max_exact = NUM_BUCKETS // 2
    nf = jnp.maximum(n, 1).astype(jnp.float32)
    large = max_exact + (jnp.log(nf / max_exact) / math.log(REL_MAX_DIST / max_exact)
                         * (NUM_BUCKETS - max_exact)).astype(jnp.int32)
    large = jnp.minimum(large, NUM_BUCKETS - 1)
    return jnp.where(n < max_exact, n, large)


def rel_bias(table, n):
    return jnp.moveaxis(table.astype(jnp.float32)[rel_bucket(jnp.maximum(n, 0))], -1, 0)


def diff_lambda(lam_p, layer):
    lp = lam_p.astype(jnp.float32)
    lam_init = 0.8 - 0.6 * math.exp(-0.3 * layer)
    lam = jnp.exp(jnp.sum(lp[0] * lp[1])) - jnp.exp(jnp.sum(lp[2] * lp[3])) + lam_init
    return lam, lam_init


def even_project(h, w_in, b_f, a_qn, a_kn, b_qn, b_kn):
    N, T, _ = h.shape
    p = h @ w_in
    sizes = [A_Q_W, A_KV_W, A_KV_W, B_Q_W, B_KV_W, B_KV_W, B_HEADS]
    aq, ak, av, bq, bk, bv, fl = jnp.split(p, np.cumsum(sizes)[:-1].tolist(), axis=-1)
    aq = rms_norm(aq.reshape(N, T, A_KVH, A_G, 2, HD), a_qn)
    ak = rms_norm(ak.reshape(N, T, A_KVH, 2, HD), a_kn).reshape(N, T, A_KVH, 2 * HD)
    av = av.reshape(N, T, A_KVH, 2 * HD)
    bq = rms_norm(bq.reshape(N, T, B_KVH, B_G, HD), b_qn)
    bk = rms_norm(bk.reshape(N, T, B_KVH, HD), b_kn)
    bv = bv.reshape(N, T, B_KVH, HD)
    logf = jax.nn.log_sigmoid(fl.astype(jnp.float32) + b_f.astype(jnp.float32))
    return aq, ak, av, bq, bk, bv, logf


def diff_attn(q, k, v, bias, mask, lam):
    N, S = k.shape[:2]
    Tq = q.shape[1]
    k = k.reshape(N, S, A_KVH, 2, HD)
    s = jnp.einsum('nqhgcd,nshcd->nhgcqs', q, k, preferred_element_type=jnp.float32) * (HD ** -0.5)
    s = s + bias.reshape(1, A_KVH, A_G, 1, Tq, S)
    p = jax.nn.softmax(jnp.where(mask, s, NEG), axis=-1)
    w = p[:, :, :, 0] - lam * p[:, :, :, 1]
    o = jnp.einsum('nhgqs,nshe->nqhge', w.astype(v.dtype), v)
    return o.reshape(N, Tq, A_HEADS, 2 * HD)


def forget_attn(q, k, v, bias, mask):
    N, S = k.shape[:2]
    Tq = q.shape[1]
    s = jnp.einsum('nqhgd,nshd->nhgqs', q, k, preferred_element_type=jnp.float32) * (HD ** -0.5)
    s = s + bias.reshape(N, B_KVH, B_G, Tq, S)
    p = jax.nn.softmax(jnp.where(mask, s, NEG), axis=-1)
    o = jnp.einsum('nhgqs,nshd->nqhgd', p.astype(v.dtype), v)
    return o.reshape(N, Tq, B_HEADS, HD)


def sink_attn(q, k, v, bias, mask, sinks):
    N, S = k.shape[:2]
    Tq = q.shape[1]
    s = jnp.einsum('nqhgd,nshd->nhgqs', q, k, preferred_element_type=jnp.float32) * (C_HD ** -0.5)
    s = jnp.where(mask, s + bias.reshape(1, C_KVH, C_G, Tq, S), NEG)
    sink = sinks.astype(jnp.float32).reshape(1, C_KVH, C_G, 1, 1)
    m = jnp.maximum(jnp.max(s, axis=-1, keepdims=True), sink)
    e = jnp.exp(s - m)
    w = e / (jnp.sum(e, axis=-1, keepdims=True) + jnp.exp(sink - m))
    o = jnp.einsum('nhgqs,nshd->nqhgd', w.astype(v.dtype), v)
    return o.reshape(N, Tq, C_HEADS, C_HD)


def even_output(oa, ob, subln, lam_init, w_out):
    N, T = oa.shape[:2]
    oa = rms_norm(oa, subln) * (1.0 - lam_init)
    o = jnp.concatenate([oa.reshape(N, T, A_OUT_W), ob.reshape(N, T, B_OUT_W)], axis=-1)
    return o @ w_out


def even_prompt(h, proj, table_a, lam):
    aq, ak, av, bq, bk, bv, logf = even_project(h, *proj)
    N, T = h.shape[:2]
    nb = T // QB
    c = jnp.cumsum(logf, axis=1)
    c_k = jnp.moveaxis(c, 1, 2)
    pos_k = jnp.arange(T)

    def block(args):
        q_a, q_b, c_q, start = args
        n = (start + jnp.arange(QB))[:, None] - pos_k[None, :]
        mask = n >= 0
        oa = diff_attn(q_a, ak, av, rel_bias(table_a, n), mask, lam)
        fbias = jnp.moveaxis(c_q, 1, 2)[..., None] - c_k[:, :, None, :]
        ob = forget_attn(q_b, bk, bv, fbias, mask)
        return oa, ob

    to_blocks = lambda t: jnp.moveaxis(t.reshape((N, nb, QB) + t.shape[2:]), 1, 0)
    from_blocks = lambda t: jnp.moveaxis(t, 0, 1).reshape((N, T) + t.shape[3:])
    oa, ob = lax.map(block, (to_blocks(aq), to_blocks(bq), to_blocks(c), jnp.arange(nb) * QB))
    return from_blocks(oa), from_blocks(ob), (ak, av, bk, bv, logf)


def even_sample(h, past, proj, table_a, lam):
    pa_k, pa_v, pb_k, pb_v, pb_f = past
    aq, ak, av, bq, bk, bv, logf = even_project(h, *proj)
    DS = h.shape[1]
    P = pa_k.shape[1]
    k_a = jnp.concatenate([pa_k, ak], axis=1)
    v_a = jnp.concatenate([pa_v, av], axis=1)
    k_b = jnp.concatenate([pb_k, bk], axis=1)
    v_b = jnp.concatenate([pb_v, bv], axis=1)
    c = jnp.cumsum(jnp.concatenate([pb_f.astype(jnp.float32), logf], axis=1), axis=1)
    c_k = jnp.moveaxis(c, 1, 2)
    n = (P + jnp.arange(DS))[:, None] - jnp.arange(P + DS)[None, :]
    mask = n >= 0
    oa = diff_attn(aq, k_a, v_a, rel_bias(table_a, n), mask, lam)
    ob = forget_attn(bq, k_b, v_b, c_k[:, :, P:, None] - c_k[:, :, None, :], mask)
    return oa, ob, (ak, av, bk, bv, logf)


def odd_project(h, w_in, qn, kn):
    N, T, _ = h.shape
    p = h @ w_in
    q, k, v = jnp.split(p, [C_Q_W, C_Q_W + C_KV_W], axis=-1)
    q = rms_norm(q.reshape(N, T, C_KVH, C_G, C_HD), qn)
    k = rms_norm(k.reshape(N, T, C_KVH, C_HD), kn)
    v = v.reshape(N, T, C_KVH, C_HD)
    return q, k, v


def odd_prompt(h, w_in, qn, kn, sinks, table_c, c_buf):
    q, k, v = odd_project(h, w_in, qn, kn)
    N, T = h.shape[:2]
    W = WINDOW
    nb = T // W
    pad = ((0, 0), (W, 0), (0, 0), (0, 0))
    kp = jnp.pad(k, pad).reshape(N, nb + 1, W, C_KVH, C_HD)
    vp = jnp.pad(v, pad).reshape(N, nb + 1, W, C_KVH, C_HD)
    k_band = jnp.moveaxis(jnp.concatenate([kp[:, :-1], kp[:, 1:]], axis=2), 1, 0)
    v_band = jnp.moveaxis(jnp.concatenate([vp[:, :-1], vp[:, 1:]], axis=2), 1, 0)
    i = jnp.arange(W)[:, None]
    j = jnp.arange(2 * W)[None, :]
    n = i + W - j
    bias = rel_bias(table_c, n)
    band = (n >= 0) & (n <= WINDOW)
    mask = band[None] & ((jnp.arange(nb)[:, None, None] * W - W + j[None]) >= 0)
    q_blk = jnp.moveaxis(q.reshape((N, nb, W) + q.shape[2:]), 1, 0)

    def block(args):
        qb, kb, vb, mb = args
        return sink_attn(qb, kb, vb, bias, mb, sinks)

    o = lax.map(block, (q_blk, k_band, v_band, mask))
    o = jnp.moveaxis(o, 0, 1).reshape(N, T, C_OUT_W)
    return o, (k[:, T - c_buf:], v[:, T - c_buf:])


def odd_sample(h, buf_k, buf_v, w_in, qn, kn, sinks, table_c, past_len):
    q, k, v = odd_project(h, w_in, qn, kn)
    DB, DS = h.shape[:2]
    c_buf = buf_k.shape[1]
    k_all = jnp.concatenate([buf_k, k], axis=1)
    v_all = jnp.concatenate([buf_v, v], axis=1)
    pos_k = (past_len - c_buf) + jnp.arange(c_buf + DS)
    n = (past_len + jnp.arange(DS))[:, None] - pos_k[None, :]
    mask = (n >= 0) & (n <= WINDOW)
    o = sink_attn(q, k_all, v_all, rel_bias(table_c, n), mask, sinks).reshape(DB, DS, C_OUT_W)
    return o, (k_all[:, DS:], v_all[:, DS:])


def mem_kv(mem, g, w_kv, kn):
    N, M, _ = mem.shape
    p = rms_norm(mem, g) @ w_kv
    k = rms_norm(p[..., :M_W].reshape(N, M, M_HEADS, HD), kn)
    v = p[..., M_W:].reshape(N, M, M_HEADS, HD)
    return k, v


def cross_attn(h, mk, mv, w_q, qn, w_o):
    N, T, _ = h.shape
    q = rms_norm((h @ w_q).reshape(N, T, M_HEADS, HD), qn)
    s = jnp.einsum('nqhd,nmhd->nhqm', q, mk, preferred_element_type=jnp.float32) * (HD ** -0.5)
    p = jax.nn.softmax(s, axis=-1)
    o = jnp.einsum('nhqm,nmhd->nqhd', p.astype(mv.dtype), mv).reshape(N, T, M_W)
    return o @ w_o


def swiglu(h, wg, wu, wd):
    return (jax.nn.silu(h @ wg) * (h @ wu)) @ wd


def gather_pages(cache, e, page_table):
    g = cache[e, page_table]
    return g.reshape((g.shape[0], g.shape[1] * g.shape[2]) + g.shape[3:])


def setup_inputs(seed: int = 0) -> dict:
    key = jax.random.key(seed)
    ks = list(jax.random.split(key, 48))
    f32 = jnp.float32

    def nrm(shape, scale=1.0):
        return jax.random.normal(ks.pop(), shape, f32) * scale

    def gain(shape):
        return 1.0 + 0.05 * nrm(shape)

    n_pages = PAST_LEN // PAGE_SIZE
    n_pool = (DEC_BATCH * n_pages * 5) // 4
    c_buf = min(WINDOW, PAST_LEN)
    page_table = jax.random.permutation(ks.pop(), n_pool)[:DEC_BATCH * n_pages]
    page_table = page_table.reshape(DEC_BATCH, n_pages).astype(jnp.int32)
    d_in = D_MODEL ** -0.5
    return dict(
        x_prompt=nrm((BATCH, SEQ, D_MODEL)),
        x_sample=nrm((DEC_BATCH, DEC_SEQ, D_MODEL)),
        mem_prompt=nrm((BATCH, N_MEM, D_MODEL)),
        cache_a_k=nrm((N_EVEN, n_pool, PAGE_SIZE, A_KVH, 2 * HD)),
        cache_a_v=nrm((N_EVEN, n_pool, PAGE_SIZE, A_KVH, 2 * HD)),
        cache_b_k=nrm((N_EVEN, n_pool, PAGE_SIZE, B_KVH, HD)),
        cache_b_v=nrm((N_EVEN, n_pool, PAGE_SIZE, B_KVH, HD)),
        cache_b_logf=jax.nn.log_sigmoid(2.0 + nrm((N_EVEN, n_pool, PAGE_SIZE, B_HEADS))),
        cache_c_k=nrm((N_ODD, DEC_BATCH, c_buf, C_KVH, C_HD)),
        cache_c_v=nrm((N_ODD, DEC_BATCH, c_buf, C_KVH, C_HD)),
        cache_m_k=nrm((DEPTH, DEC_BATCH, N_MEM, M_HEADS, HD)),
        cache_m_v=nrm((DEPTH, DEC_BATCH, N_MEM, M_HEADS, HD)),
        page_table=page_table,
        rel_table=nrm((NUM_BUCKETS, A_HEADS + C_HEADS), 0.1),
        norm_mix=gain((DEPTH, D_MODEL)),
        norm_cross=gain((DEPTH, D_MODEL)),
        norm_mem=gain((DEPTH, D_MODEL)),
        norm_ffn=gain((DEPTH, D_MODEL)),
        w_in_even=nrm((N_EVEN, D_MODEL, EVEN_IN), d_in),
        b_forget=2.0 + 0.1 * nrm((N_EVEN, B_HEADS)),
        a_q_norm=gain((N_EVEN, 2, HD)),
        a_k_norm=gain((N_EVEN, 2, HD)),
        a_lambda=nrm((N_EVEN, 4, HD), 0.1),
        a_subln=gain((N_EVEN, 2 * HD)),
        b_q_norm=gain((N_EVEN, HD)),
        b_k_norm=gain((N_EVEN, HD)),
        w_out_even=nrm((N_EVEN, EVEN_OUT, D_MODEL), EVEN_OUT ** -0.5),
        w_in_odd=nrm((N_ODD, D_MODEL, ODD_IN), d_in),
        c_q_norm=gain((N_ODD, C_HD)),
        c_k_norm=gain((N_ODD, C_HD)),
        c_sinks=nrm((N_ODD, C_HEADS), 0.5),
        w_out_odd=nrm((N_ODD, C_OUT_W, D_MODEL), C_OUT_W ** -0.5),
        w_mq=nrm((DEPTH, D_MODEL, M_W), d_in),
        w_mkv=nrm((DEPTH, D_MODEL, 2 * M_W), d_in),
        m_q_norm=gain((DEPTH, HD)),
        m_k_norm=gain((DEPTH, HD)),
        w_mo=nrm((DEPTH, M_W, D_MODEL), M_W ** -0.5),
        w_gate=nrm((DEPTH, D_MODEL, D_FF), d_in),
        w_up=nrm((DEPTH, D_MODEL, D_FF), d_in),
        w_down=nrm((DEPTH, D_FF, D_MODEL), D_FF ** -0.5),
    )


def reference(x_prompt, x_sample, mem_prompt, cache_a_k, cache_a_v, cache_b_k, cache_b_v,
              cache_b_logf, cache_c_k, cache_c_v, cache_m_k, cache_m_v, page_table, rel_table,
              norm_mix, norm_cross, norm_mem, norm_ffn, w_in_even, b_forget, a_q_norm, a_k_norm,
              a_lambda, a_subln, b_q_norm, b_k_norm, w_out_even, w_in_odd, c_q_norm, c_k_norm,
              c_sinks, w_out_odd, w_mq, w_mkv, m_q_norm, m_k_norm, w_mo, w_gate, w_up, w_down):
    past_len = page_table.shape[1] * cache_a_k.shape[2]
    c_buf = cache_c_k.shape[2]
    table_a = rel_table[:, :A_HEADS]
    table_c = rel_table[:, A_HEADS:]
    xp, xs = x_prompt, x_sample
    pa_k, sa_k, pa_v, sa_v, pb_k, sb_k, pb_v, sb_v, pb_f, sb_f = [], [], [], [], [], [], [], [], [], []
    pc_k, sc_k, pc_v, sc_v, pm_k, pm_v = [], [], [], [], [], []
    for l in range(DEPTH):
        hp = rms_norm(xp, norm_mix[l])
        hs = rms_norm(xs, norm_mix[l])
        if l % 2 == 0:
            e = l // 2
            lam, lam_init = diff_lambda(a_lambda[e], l)
            proj = (w_in_even[e], b_forget[e], a_q_norm[e], a_k_norm[e], b_q_norm[e], b_k_norm[e])
            oa, ob, (ak, av, bk, bv, lf) = even_prompt(hp, proj, table_a, lam)
            xp = xp + even_output(oa, ob, a_subln[e], lam_init, w_out_even[e])
            pa_k.append(ak); pa_v.append(av); pb_k.append(bk); pb_v.append(bv)
            pb_f.append(lf.astype(x_prompt.dtype))
            past = (gather_pages(cache_a_k, e, page_table), gather_pages(cache_a_v, e, page_table),
                    gather_pages(cache_b_k, e, page_table), gather_pages(cache_b_v, e, page_table),
                    gather_pages(cache_b_logf, e, page_table))
            oa, ob, (ak, av, bk, bv, lf) = even_sample(hs, past, proj, table_a, lam)
            xs = xs + even_output(oa, ob, a_subln[e], lam_init, w_out_even[e])
            sa_k.append(ak); sa_v.append(av); sb_k.append(bk); sb_v.append(bv)
            sb_f.append(lf.astype(x_sample.dtype))
        else:
            o_ = l // 2
            o, (ck, cv) = odd_prompt(hp, w_in_odd[o_], c_q_norm[o_], c_k_norm[o_], c_sinks[o_],
                                     table_c, c_buf)
            xp = xp + o @ w_out_odd[o_]
            pc_k.append(ck); pc_v.append(cv)
            o, (ck, cv) = odd_sample(hs, cache_c_k[o_], cache_c_v[o_], w_in_odd[o_], c_q_norm[o_],
                                     c_k_norm[o_], c_sinks[o_], table_c, past_len)
            xs = xs + o @ w_out_odd[o_]
            sc_k.append(ck); sc_v.append(cv)
        mk, mv = mem_kv(mem_prompt, norm_mem[l], w_mkv[l], m_k_norm[l])
        pm_k.append(mk); pm_v.append(mv)
        xp = xp + cross_attn(rms_norm(xp, norm_cross[l]), mk, mv, w_mq[l], m_q_norm[l], w_mo[l])
        xs = xs + cross_attn(rms_norm(xs, norm_cross[l]), cache_m_k[l], cache_m_v[l], w_mq[l],
                             m_q_norm[l], w_mo[l])
        xp = xp + swiglu(rms_norm(xp, norm_ffn[l]), w_gate[l], w_up[l], w_down[l])
        xs = xs + swiglu(rms_norm(xs, norm_ffn[l]), w_gate[l], w_up[l], w_down[l])
    new_a_k_p = jnp.stack(pa_k)
    new_a_k_s = jnp.stack(sa_k)
    new_a_v_p = jnp.stack(pa_v)
    new_a_v_s = jnp.stack(sa_v)
    new_b_k_p = jnp.stack(pb_k)
    new_b_k_s = jnp.stack(sb_k)
    new_b_v_p = jnp.stack(pb_v)
    new_b_v_s = jnp.stack(sb_v)
    new_b_logf_p = jnp.stack(pb_f)
    new_b_logf_s = jnp.stack(sb_f)
    new_c_k_p = jnp.stack(pc_k)
    new_c_k_s = jnp.stack(sc_k)
    new_c_v_p = jnp.stack(pc_v)
    new_c_v_s = jnp.stack(sc_v)
    new_m_k_p = jnp.stack(pm_k)
    new_m_v_p = jnp.stack(pm_v)
    return (xp, xs, new_a_k_p, new_a_k_s, new_a_v_p, new_a_v_s, new_b_k_p, new_b_k_s,
            new_b_v_p, new_b_v_s, new_b_logf_p, new_b_logf_s, new_c_k_p, new_c_k_s,
            new_c_v_p, new_c_v_s, new_m_k_p, new_m_v_p)
```

```python
import functools
import math

import jax
import jax.numpy as jnp
from jax import lax
from jax.experimental import pallas as pl
from jax.experimental.pallas import tpu as pltpu

F32 = jnp.float32
_CD = jnp.bfloat16

HD = 128
A_HEADS, A_KVH = 8, 4
A_G = A_HEADS // A_KVH
B_HEADS, B_KVH = 16, 4
B_G = B_HEADS // B_KVH
C_HEADS, C_KVH, C_HD = 64, 8, 64
C_G = C_HEADS // C_KVH
WINDOW = 128
M_HEADS = 4
NUM_BUCKETS = 32
REL_MAX_DIST = 128
A_Q_W = A_HEADS * 2 * HD
A_KV_W = A_KVH * 2 * HD
B_Q_W = B_HEADS * HD
B_KV_W = B_KVH * HD
EVEN_MAIN = A_Q_W + 2 * A_KV_W + B_Q_W + 2 * B_KV_W
C_Q_W = C_HEADS * C_HD
C_KV_W = C_KVH * C_HD
ODD_IN = C_Q_W + 2 * C_KV_W
M_W = M_HEADS * HD
NEG = -1e30
M_INIT = -1e38
EPS = 1e-6
SCALE = HD ** -0.5
C_SCALE = C_HD ** -0.5

LANES = 128
VMEM_LIMIT = 60 * 1024 * 1024
TQ_A, TK_A = 128, 256
PAGES_PER_STEP = 4


def _cparams(n_axes):
    return pltpu.CompilerParams(dimension_semantics=("arbitrary",) * n_axes,
                                vmem_limit_bytes=VMEM_LIMIT)


def _dot(a, b):
    return jnp.dot(a, b, preferred_element_type=F32)


def _dot_nt(a, b):
    return lax.dot_general(a, b, (((1,), (1,)), ((), ())), preferred_element_type=F32)


def _rep(x, width):
    n = width // LANES
    return x if n == 1 else pltpu.repeat(x, n, axis=1)


def _rmsnorm_kernel(x_ref, g_ref, o_ref):
    x = x_ref[...]
    ms = jnp.mean(x * x, axis=-1, keepdims=True)
    o_ref[...] = (x * lax.rsqrt(ms + EPS) * g_ref[...]).astype(o_ref.dtype)


def _rmsnorm(x, g):
    m, d = x.shape
    bm = math.gcd(m, 256)
    return pl.pallas_call(
        _rmsnorm_kernel,
        grid=(m // bm,),
        in_specs=[pl.BlockSpec((bm, d), lambda i: (i, 0)),
                  pl.BlockSpec((1, d), lambda i: (0, 0))],
        out_specs=pl.BlockSpec((bm, d), lambda i: (i, 0)),
        out_shape=jax.ShapeDtypeStruct((m, d), _CD),
        compiler_params=_cparams(1),
        name="rmsnorm",
    )(x, g.reshape(1, d).astype(F32))


def _group_rms(x, gs):
    x2 = x * x
    if gs == LANES:
        return lax.rsqrt(jnp.mean(x2, axis=-1, keepdims=True) + EPS)
    lo = lax.broadcasted_iota(jnp.int32, x.shape, 1) < gs
    s_lo = jnp.sum(jnp.where(lo, x2, 0.0), axis=-1, keepdims=True)
    s_hi = jnp.sum(jnp.where(lo, 0.0, x2), axis=-1, keepdims=True)
    return jnp.where(lo, lax.rsqrt(s_lo / gs + EPS), lax.rsqrt(s_hi / gs + EPS))


def _epilogue(accs, vec_ref, epi, gs):
    if epi == "none":
        return accs[0]
    if epi == "silu_mul":
        g, u = accs
        return g * (1.0 / (1.0 + jnp.exp(-g))) * u
    y = accs[0]
    if epi == "logsig":
        z = y + vec_ref[2:3, :]
        ls = jnp.minimum(z, 0.0) - jnp.log(1.0 + jnp.exp(-jnp.abs(z)))
        return jnp.where(vec_ref[1:2, :] > 0, ls, 0.0)
    assert epi == "norm"
    cols = []
    for c in range(y.shape[1] // LANES):
        sl = slice(c * LANES, (c + 1) * LANES)
        x = y[:, sl]
        xn = x * _group_rms(x, gs) * vec_ref[0:1, sl]
        cols.append(jnp.where(vec_ref[1:2, sl] > 0, xn, x))
    return cols[0] if len(cols) == 1 else jnp.concatenate(cols, axis=1)


def _mm_kernel(*refs, n_w, cast_w, has_s, epi, has_res, n_out, gs, i_axis, cast_rows, k_perm):
    it = iter(refs)
    xp = next(it)
    xs = next(it) if has_s else None
    ws = [next(it) for _ in range(n_w)]
    vec = next(it) if epi in ("norm", "logsig") else None
    rp = next(it) if has_res else None
    rs = next(it) if (has_res and has_s) else None
    outs_p = [next(it) for _ in range(n_out)]
    outs_s = [next(it) for _ in range(n_out)] if has_s else []
    wbs = [next(it) for _ in range(n_w)] if cast_w else ws
    i = pl.program_id(i_axis)

    if cast_w:
        @pl.when(i == 0)
        def _():
            for w, wb in zip(ws, wbs):
                for cb in range(w.shape[0] // cast_rows):
                    src = k_perm[cb] if k_perm is not None else cb
                    wb[cb * cast_rows:(cb + 1) * cast_rows, :] = (
                        w[src * cast_rows:(src + 1) * cast_rows, :].astype(_CD))

    def emit(x_ref, r_ref, o_refs):
        y = _epilogue([_dot(x_ref[...], wb[...]) for wb in wbs], vec, epi, gs)
        if r_ref is not None:
            y = y + r_ref[...]
        for o in o_refs:
            o[...] = y.astype(o.dtype)

    emit(xp, rp, outs_p)
    if has_s:
        @pl.when(i == 0)
        def _():
            emit(xs, rs, outs_s)


def _cast_rows(k):
    for r in (512, 256, 128, 64, 32, 16):
        if k % r == 0:
            return r
    raise ValueError(k)


def _mm(xp, xs, ws, layer, n_cols, *, bm, bn, epi="none", vec=None, res=None,
        out_dtypes=(F32,), gs=LANES, k_perm=None, order="ws"):
    mp, k = xp.shape
    has_s = xs is not None
    cast_w = order == "ws"
    assert cast_w or (ws[0].dtype == _CD and not has_s)
    bm = min(bm, mp)
    bn = min(bn, n_cols)
    assert mp % bm == 0 and n_cols % bn == 0
    ni, nj = mp // bm, n_cols // bn
    if order == "ws":
        grid, i_axis = (nj, ni), 1
        ij = lambda a, b: (b, a)
    else:
        grid, i_axis = (ni, nj), 0
        ij = lambda a, b: (a, b)
    cast_rows = 512 if k_perm is not None else _cast_rows(k)

    def xmap(a, b):
        return (ij(a, b)[0], 0)

    def wmap(a, b):
        return (layer, 0, ij(a, b)[1])

    def omap(a, b):
        return ij(a, b)

    def smap(a, b):
        return (0, ij(a, b)[1])

    in_specs = [pl.BlockSpec((bm, k), xmap)]
    args = [xp]
    if has_s:
        ms = xs.shape[0]
        in_specs.append(pl.BlockSpec((ms, k), lambda a, b: (0, 0)))
        args.append(xs)
    for w in ws:
        in_specs.append(pl.BlockSpec((None, k, bn), wmap))
        args.append(w)
    if vec is not None:
        in_specs.append(pl.BlockSpec((8, bn), smap))
        args.append(vec)
    has_res = res is not None
    if has_res:
        in_specs.append(pl.BlockSpec((bm, bn), omap))
        args.append(res[0])
        if has_s:
            in_specs.append(pl.BlockSpec((ms, bn), smap))
            args.append(res[1])
    out_specs, out_shape = [], []
    for dt in out_dtypes:
        out_specs.append(pl.BlockSpec((bm, bn), omap))
        out_shape.append(jax.ShapeDtypeStruct((mp, n_cols), dt))
    if has_s:
        for dt in out_dtypes:
            out_specs.append(pl.BlockSpec((ms, bn), smap))
            out_shape.append(jax.ShapeDtypeStruct((ms, n_cols), dt))
    scratch = [pltpu.VMEM((k, bn), _CD) for _ in ws] if cast_w else []
    kern = functools.partial(
        _mm_kernel, n_w=len(ws), cast_w=cast_w, has_s=has_s, epi=epi, has_res=has_res,
        n_out=len(out_dtypes), gs=gs, i_axis=i_axis, cast_rows=cast_rows, k_perm=k_perm)
    return pl.pallas_call(
        kern, grid=grid, in_specs=in_specs, out_specs=out_specs, out_shape=out_shape,
        scratch_shapes=scratch, compiler_params=_cparams(2), name="proj_" + epi,
    )(*args)


def _vec(n, gain=None, flag=None, bias=None):
    v = jnp.zeros((8, n), F32)
    if gain is not None:
        v = v.at[0].set(gain.astype(F32))
    if flag is not None:
        v = v.at[1].set(flag.astype(F32))
    if bias is not None:
        v = v.at[2].set(bias.astype(F32))
    return v


def _rel_bucket(n):
    max_exact = NUM_BUCKETS // 2
    nf = jnp.maximum(n, 1).astype(F32)
    large = max_exact + (jnp.log(nf / max_exact) / math.log(REL_MAX_DIST / max_exact)
                         * (NUM_BUCKETS - max_exact)).astype(jnp.int32)
    large = jnp.minimum(large, NUM_BUCKETS - 1)
    return jnp.where(n < max_exact, n, large)


def _rel_bias(table, n):
    return jnp.moveaxis(table.astype(F32)[_rel_bucket(jnp.maximum(n, 0))], -1, 0)


def _cumsum_kernel(x_ref, o_ref, *, blk):
    t = x_ref.shape[0]
    r = lax.broadcasted_iota(jnp.int32, (blk, blk), 0)
    c = lax.broadcasted_iota(jnp.int32, (blk, blk), 1)
    tri = (c <= r).astype(F32)

    def body(b, carry):
        rows = pl.ds(pl.multiple_of(b * blk, blk), blk)
        y = jnp.dot(tri, x_ref[rows, :], preferred_element_type=F32,
                    precision=lax.Precision.HIGHEST) + carry
        o_ref[rows, :] = y
        return y[blk - 1:blk, :]

    lax.fori_loop(0, t // blk, body, jnp.zeros((1, LANES), F32))


def _cumsum_time(x):
    b, t, w = x.shape
    blk = math.gcd(t, 256)
    return pl.pallas_call(
        functools.partial(_cumsum_kernel, blk=blk),
        grid=(b,),
        in_specs=[pl.BlockSpec((None, t, w), lambda n: (n, 0, 0))],
        out_specs=pl.BlockSpec((None, t, w), lambda n: (n, 0, 0)),
        out_shape=jax.ShapeDtypeStruct((b, t, w), F32),
        compiler_params=_cparams(1),
        name="logf_cumsum",
    )(x)


def _diff_lambda(lam_ref, lam_init):
    la = lam_ref[...]
    e1 = jnp.exp(jnp.sum(la[0:1, :] * la[1:2, :], axis=-1, keepdims=True))
    e2 = jnp.exp(jnp.sum(la[2:3, :] * la[3:4, :], axis=-1, keepdims=True))
    return e1 - e2 + lam_init


def _subln(o, sub_ref, lam_init):
    ms = jnp.mean(o * o, axis=-1, keepdims=True)
    return o * lax.rsqrt(ms + EPS) * sub_ref[...] * (1.0 - lam_init)


def _online_update(s, v, m_ref, l_ref, acc_ref, sidx, aidx):
    m_prev = m_ref[sidx]
    m_new = jnp.maximum(m_prev, jnp.max(s, axis=1, keepdims=True))
    p = jnp.exp(s - _rep(m_new, s.shape[1]))
    alpha = jnp.exp(m_prev - m_new)
    l_ref[sidx] = alpha * l_ref[sidx] + jnp.sum(p, axis=1, keepdims=True)
    m_ref[sidx] = m_new
    acc_ref[aidx] = acc_ref[aidx] * _rep(alpha, v.shape[1]) + _dot(p.astype(_CD), v)


def _even_prompt_kernel(aq_ref, ak_ref, av_ref, bq_ref, bk_ref, bv_ref, cq_ref, ck_ref,
                        abias_ref, fmask_ref, lam_ref, sub_ref, o_ref,
                        m_ref, l_ref, acca_ref, accb_ref, *, lam_init):
    tq, tk = TQ_A, TK_A
    qi = pl.program_id(2)
    jd = (qi * tq) // tk
    m_ref[...] = jnp.full(m_ref.shape, M_INIT, F32)
    l_ref[...] = jnp.zeros(l_ref.shape, F32)
    acca_ref[...] = jnp.zeros(acca_ref.shape, F32)
    accb_ref[...] = jnp.zeros(accb_ref.shape, F32)

    def step(j, diag):
        ks = pl.ds(pl.multiple_of(j * tk, tk), tk)
        ka = ak_ref[ks, :]
        va = av_ref[ks, :]
        off = jnp.minimum((qi * tq - j * tk) // LANES, 3)
        for g in range(A_G):
            bias = abias_ref[g, off]
            for c in range(2):
                q = aq_ref[:, (g * 2 + c) * HD:(g * 2 + c + 1) * HD]
                s = _dot_nt(q, ka[:, c * HD:(c + 1) * HD]) * SCALE + bias
                _online_update(s, va, m_ref, l_ref, acca_ref, g * 2 + c, g * 2 + c)
        kb = bk_ref[ks, :]
        vb = bv_ref[ks, :]
        for g in range(B_G):
            q = bq_ref[:, g * HD:(g + 1) * HD]
            s = _dot_nt(q, kb) * SCALE + (cq_ref[:, g:g + 1] - ck_ref[g, pl.ds(j, 1), :])
            if diag:
                s = s + fmask_ref[off]
            _online_update(s, vb, m_ref, l_ref, accb_ref, 4 + g, g)

    def body(j, carry):
        step(j, False)
        return carry

    lax.fori_loop(0, jd, body, 0)
    step(jd, True)

    lam = _diff_lambda(lam_ref, lam_init)
    for g in range(A_G):
        o1 = acca_ref[2 * g] / _rep(l_ref[2 * g], 2 * HD)
        o2 = acca_ref[2 * g + 1] / _rep(l_ref[2 * g + 1], 2 * HD)
        o = _subln(o1 - lam * o2, sub_ref, lam_init)
        o_ref[:, g * 2 * HD:(g + 1) * 2 * HD] = o.astype(o_ref.dtype)
    for g in range(B_G):
        o = accb_ref[g] / l_ref[4 + g]
        o_ref[:, A_G * 2 * HD + g * HD:A_G * 2 * HD + (g + 1) * HD] = o.astype(o_ref.dtype)


def _even_prompt_attn(pb, cq, ck, abias, fmask, lam_p, subln, lam_init):
    b, t, _ = pb.shape
    tq, tk = TQ_A, TK_A
    nq, nkb = t // tq, t // tk
    o_w = A_G * 2 * HD + B_G * HD
    ak0 = A_Q_W // (2 * HD)
    av0 = (A_Q_W + A_KV_W) // (2 * HD)
    bq0 = (A_Q_W + 2 * A_KV_W) // (B_G * HD)
    bk0 = (A_Q_W + 2 * A_KV_W + B_Q_W) // HD
    bv0 = bk0 + B_KVH
    in_specs = [
        pl.BlockSpec((None, tq, A_G * 2 * HD), lambda n, h, q: (n, q, h)),
        pl.BlockSpec((None, t, 2 * HD), lambda n, h, q: (n, 0, ak0 + h)),
        pl.BlockSpec((None, t, 2 * HD), lambda n, h, q: (n, 0, av0 + h)),
        pl.BlockSpec((None, tq, B_G * HD), lambda n, h, q: (n, q, bq0 + h)),
        pl.BlockSpec((None, t, HD), lambda n, h, q: (n, 0, bk0 + h)),
        pl.BlockSpec((None, t, HD), lambda n, h, q: (n, 0, bv0 + h)),
        pl.BlockSpec((None, None, tq, B_G), lambda n, h, q: (n, h, q, 0)),
        pl.BlockSpec((None, None, B_G, nkb, tk), lambda n, h, q: (n, h, 0, 0, 0)),
        pl.BlockSpec((A_G, 4, tq, tk), lambda n, h, q: (h, 0, 0, 0)),
        pl.BlockSpec((2, tq, tk), lambda n, h, q: (0, 0, 0)),
        pl.BlockSpec((4, HD), lambda n, h, q: (0, 0)),
        pl.BlockSpec((1, 2 * HD), lambda n, h, q: (0, 0)),
    ]
    return pl.pallas_call(
        functools.partial(_even_prompt_kernel, lam_init=lam_init),
        grid=(b, A_KVH, nq),
        in_specs=in_specs,
        out_specs=pl.BlockSpec((None, tq, o_w), lambda n, h, q: (n, q, h)),
        out_shape=jax.ShapeDtypeStruct((b, t, A_KVH * o_w), _CD),
        scratch_shapes=[pltpu.VMEM((8, tq, LANES), F32), pltpu.VMEM((8, tq, LANES), F32),
                        pltpu.VMEM((4, tq, 2 * HD), F32), pltpu.VMEM((4, tq, HD), F32)],
        compiler_params=_cparams(3),
        name="even_prompt_attn",
    )(pb, pb, pb, pb, pb, pb, cq, ck, abias, fmask, lam_p, subln)


def _even_decode_kernel(pt_ref, qa_ref, qb_ref, kan_ref, van_ref, kbn_ref, vbn_ref, lfn_ref,
                        abias_ref, aself_ref, lam_ref, sub_ref, *rest, n_chunks, lam_init):
    del pt_ref
    npg = PAGES_PER_STEP
    pages = rest[:5 * npg]
    o_ref = rest[5 * npg]
    m_ref, l_ref, acca_ref, accb_ref, run_ref = rest[5 * npg + 1:]
    c = pl.program_id(1)

    @pl.when(c == 0)
    def _():
        sa = jnp.zeros((16, 1), F32)
        sb = jnp.zeros((16, 1), F32)
        for h in range(A_KVH):
            sa = sa + jnp.sum(qa_ref[h].astype(F32) * kan_ref[h:h + 1, :].astype(F32),
                              axis=1, keepdims=True)
            sb = sb + jnp.sum(qb_ref[h].astype(F32) * kbn_ref[h:h + 1, :].astype(F32),
                              axis=1, keepdims=True)
            acca_ref[h] = jnp.broadcast_to(van_ref[h:h + 1, :].astype(F32), (16, 2 * HD))
            accb_ref[h] = jnp.broadcast_to(vbn_ref[h:h + 1, :].astype(F32), (16, HD))
        m_ref[0] = sa * SCALE + aself_ref[...]
        m_ref[1] = jnp.broadcast_to(sb * SCALE, (16, LANES))
        l_ref[...] = jnp.ones(l_ref.shape, F32)
        run_ref[...] = lfn_ref[...]

    page = pages[0].shape[0]
    r = lax.broadcasted_iota(jnp.int32, (page, page), 0)
    s_ = lax.broadcasted_iota(jnp.int32, (page, page), 1)
    later = (r > s_).astype(F32)
    ones = jnp.ones((page, page), F32)
    tdims = (((0,), (0,)), ((), ()))

    for p in range(npg):
        ka_ref, va_ref, kb_ref, vb_ref, lf_ref = pages[5 * p:5 * p + 5]
        lf = lf_ref[...]
        suffix = lax.dot_general(lf, later, tdims, preferred_element_type=F32,
                                 precision=lax.Precision.HIGHEST)
        total = lax.dot_general(lf, ones, tdims, preferred_element_type=F32,
                                precision=lax.Precision.HIGHEST)
        fbias = run_ref[...] + suffix
        run_ref[...] = run_ref[...] + total
        sa = jnp.zeros((16, page), F32)
        sb = jnp.zeros((16, page), F32)
        for h in range(A_KVH):
            sa = sa + _dot_nt(qa_ref[h], ka_ref[:, h, :].astype(_CD))
            sb = sb + _dot_nt(qb_ref[h], kb_ref[:, h, :].astype(_CD))
        sa = sa * SCALE + abias_ref[p]
        sb = sb * SCALE + fbias
        for idx, (s, v_ref, acc_ref) in enumerate(((sa, va_ref, acca_ref), (sb, vb_ref, accb_ref))):
            m_prev = m_ref[idx]
            m_new = jnp.maximum(m_prev, jnp.max(s, axis=1, keepdims=True))
            pr = jnp.exp(s - _rep(m_new, s.shape[1]))
            alpha = jnp.exp(m_prev - m_new)
            l_ref[idx] = alpha * l_ref[idx] + jnp.sum(pr, axis=1, keepdims=True)
            m_ref[idx] = m_new
            prc = pr.astype(_CD)
            for h in range(A_KVH):
                v = v_ref[:, h, :].astype(_CD)
                acc_ref[h] = acc_ref[h] * _rep(alpha, v.shape[1]) + _dot(prc, v)

    @pl.when(c == n_chunks - 1)
    def _():
        lam = _diff_lambda(lam_ref, lam_init)
        o_w = A_G * 2 * HD + B_G * HD
        la = l_ref[0]
        lb = l_ref[1]
        for h in range(A_KVH):
            for g in range(A_G):
                r1 = h * 4 + g * 2
                o1 = acca_ref[h, r1:r1 + 1, :] / _rep(la[r1:r1 + 1, :], 2 * HD)
                o2 = acca_ref[h, r1 + 1:r1 + 2, :] / _rep(la[r1 + 1:r1 + 2, :], 2 * HD)
                o = _subln(o1 - lam * o2, sub_ref, lam_init)
                o_ref[0:1, h * o_w + g * 2 * HD:h * o_w + (g + 1) * 2 * HD] = o
            for g in range(B_G):
                rr = h * 4 + g
                o = accb_ref[h, rr:rr + 1, :] / lb[rr:rr + 1, :]
                o_ref[0:1, h * o_w + A_G * 2 * HD + g * HD:h * o_w + A_G * 2 * HD + (g + 1) * HD] = o


def _even_decode_attn(e, page_table, caches, qa16, qb16, new_kv, lf_new, abias, aself,
                      lam_p, subln, lam_init):
    cache_a_k, cache_a_v, cache_b_k, cache_b_v, cache_b_logf = caches
    db, n_pages = page_table.shape
    page = cache_a_k.shape[2]
    npg = PAGES_PER_STEP
    assert n_pages % npg == 0
    n_chunks = n_pages // npg
    kan, van, kbn, vbn = new_kv

    def fixed(shape):
        nd = len(shape)
        return pl.BlockSpec((None,) + shape, lambda b, c, pt: (b,) + (0,) * nd)

    def const(shape):
        nd = len(shape)
        return pl.BlockSpec(shape, lambda b, c, pt: (0,) * nd)

    in_specs = [
        fixed((A_KVH, 16, 2 * HD)), fixed((B_KVH, 16, HD)),
        fixed((A_KVH, 2 * HD)), fixed((A_KVH, 2 * HD)), fixed((B_KVH, HD)), fixed((B_KVH, HD)),
        fixed((16, LANES)),
        pl.BlockSpec((npg, 16, page), lambda b, c, pt: (c, 0, 0)),
        const((16, LANES)), const((4, HD)), const((1, 2 * HD)),
    ]
    args = [qa16, qb16, kan, van, kbn, vbn, lf_new, abias, aself, lam_p, subln]
    for p in range(npg):
        def pmap(b, c, pt, p=p):
            return (e, pt[b, n_pages - 1 - (c * npg + p)], 0, 0, 0)

        def fmap(b, c, pt, p=p):
            return (e, pt[b, n_pages - 1 - (c * npg + p)], 0, 0)

        in_specs += [
            pl.BlockSpec((None, None, page, A_KVH, 2 * HD), pmap),
            pl.BlockSpec((None, None, page, A_KVH, 2 * HD), pmap),
            pl.BlockSpec((None, None, page, B_KVH, HD), pmap),
            pl.BlockSpec((None, None, page, B_KVH, HD), pmap),
            pl.BlockSpec((None, None, page, B_HEADS), fmap),
        ]
        args += [cache_a_k, cache_a_v, cache_b_k, cache_b_v, cache_b_logf]
    o_w = A_KVH * (A_G * 2 * HD + B_G * HD)
    out = pl.pallas_call(
        functools.partial(_even_decode_kernel, n_chunks=n_chunks, lam_init=lam_init),
        grid_spec=pltpu.PrefetchScalarGridSpec(
            num_scalar_prefetch=1,
            grid=(db, n_chunks),
            in_specs=in_specs,
            out_specs=pl.BlockSpec((None, 1, o_w), lambda b, c, pt: (b, 0, 0)),
            scratch_shapes=[pltpu.VMEM((2, 16, LANES), F32), pltpu.VMEM((2, 16, LANES), F32),
                            pltpu.VMEM((A_KVH, 16, 2 * HD), F32), pltpu.VMEM((B_KVH, 16, HD), F32),
                            pltpu.VMEM((16, LANES), F32)],
        ),
        out_shape=jax.ShapeDtypeStruct((db, 1, o_w), F32),
        compiler_params=_cparams(2),
        name="even_decode_attn",
    )(page_table, *args)
    return out.reshape(db, o_w)


def _swa_prompt_kernel(sink_ref, q_ref, kp_ref, kc_ref, vp_ref, vc_ref, bias_ref, o_ref):
    qi = pl.program_id(1)
    kp = pl.program_id(2)
    w = WINDOW
    has_prev = qi > 0
    for hh in range(2 * C_G):
        kv = hh // C_G
        ksl = slice(kv * C_HD, (kv + 1) * C_HD)
        q = q_ref[:, hh * C_HD:(hh + 1) * C_HD]
        s_prev = _dot_nt(q, kp_ref[:, ksl]) * C_SCALE + bias_ref[hh, :, 0:w]
        s_prev = jnp.where(has_prev, s_prev, NEG)
        s_cur = _dot_nt(q, kc_ref[:, ksl]) * C_SCALE + bias_ref[hh, :, w:2 * w]
        sink = sink_ref[kp * 2 * C_G + hh]
        m = jnp.maximum(jnp.maximum(jnp.max(s_prev, axis=1, keepdims=True),
                                    jnp.max(s_cur, axis=1, keepdims=True)), sink)
        e_prev = jnp.exp(s_prev - m)
        e_cur = jnp.exp(s_cur - m)
        den = (jnp.sum(e_prev, axis=1, keepdims=True) + jnp.sum(e_cur, axis=1, keepdims=True)
               + jnp.exp(sink - m))
        inv = 1.0 / den
        o = (_dot((e_prev * inv).astype(_CD), vp_ref[:, ksl])
             + _dot((e_cur * inv).astype(_CD), vc_ref[:, ksl]))
        o_ref[:, hh * C_HD:(hh + 1) * C_HD] = o.astype(o_ref.dtype)


def _swa_prompt_attn(pb, bias, sinks):
    b, t, _ = pb.shape
    w = WINDOW
    nq = t // w
    qw = 2 * C_G * C_HD
    k0 = C_Q_W // LANES
    v0 = (C_Q_W + C_KV_W) // LANES
    prev = lambda q: jnp.maximum(q - 1, 0)
    in_specs = [
        pl.BlockSpec(memory_space=pltpu.SMEM),
        pl.BlockSpec((None, w, qw), lambda n, q, k: (n, q, k)),
        pl.BlockSpec((None, w, LANES), lambda n, q, k: (n, prev(q), k0 + k)),
        pl.BlockSpec((None, w, LANES), lambda n, q, k: (n, q, k0 + k)),
        pl.BlockSpec((None, w, LANES), lambda n, q, k: (n, prev(q), v0 + k)),
        pl.BlockSpec((None, w, LANES), lambda n, q, k: (n, q, v0 + k)),
        pl.BlockSpec((2 * C_G, w, 2 * w), lambda n, q, k: (k, 0, 0)),
    ]
    return pl.pallas_call(
        _swa_prompt_kernel,
        grid=(b, nq, C_KVH // 2),
        in_specs=in_specs,
        out_specs=pl.BlockSpec((None, w, qw), lambda n, q, k: (n, q, k)),
        out_shape=jax.ShapeDtypeStruct((b, t, C_Q_W), _CD),
        compiler_params=_cparams(3),
        name="swa_prompt_attn",
    )(sinks.astype(F32), pb, pb, pb, pb, pb, bias)


def _swa_decode_kernel(q_ref, kbuf_ref, vbuf_ref, kn_ref, vn_ref, bias_ref, bself_ref, sink_ref,
                       o_ref):
    nh = C_HEADS
    s = jnp.zeros((nh, kbuf_ref.shape[0]), F32)
    for kv in range(C_KVH):
        s = s + _dot_nt(q_ref[kv], kbuf_ref[:, kv, :].astype(_CD))
    s = s * C_SCALE + bias_ref[...]
    qsum = q_ref[0].astype(F32)
    for kv in range(1, C_KVH):
        qsum = qsum + q_ref[kv].astype(F32)
    s_new = jnp.sum(qsum * kn_ref[...].astype(F32), axis=1, keepdims=True) * C_SCALE + bself_ref[...]
    sink = sink_ref[...]
    m = jnp.maximum(jnp.maximum(jnp.max(s, axis=1, keepdims=True), s_new), sink)
    e = jnp.exp(s - m)
    e_new = jnp.exp(s_new - m)
    den = jnp.sum(e, axis=1, keepdims=True) + e_new + jnp.exp(sink - m)
    inv = 1.0 / den
    wgt = (e * inv).astype(_CD)
    row_kv = lax.broadcasted_iota(jnp.int32, (nh, C_HD), 0) // C_G
    o = (e_new * inv).astype(_CD).astype(F32) * vn_ref[...].astype(F32)
    for kv in range(C_KVH):
        pv = _dot(wgt, vbuf_ref[:, kv, :].astype(_CD))
        o = o + jnp.where(row_kv == kv, pv, 0.0)
    o_ref[...] = o


def _swa_decode_attn(o_idx, q64, cache_c_k, cache_c_v, kn, vn, bias, bself, sinks):
    db = q64.shape[0]
    cbuf = cache_c_k.shape[2]
    in_specs = [
        pl.BlockSpec((None, C_KVH, C_HEADS, C_HD), lambda b: (b, 0, 0, 0)),
        pl.BlockSpec((None, None, cbuf, C_KVH, C_HD), lambda b: (o_idx, b, 0, 0, 0)),
        pl.BlockSpec((None, None, cbuf, C_KVH, C_HD), lambda b: (o_idx, b, 0, 0, 0)),
        pl.BlockSpec((None, C_HEADS, C_HD), lambda b: (b, 0, 0)),
        pl.BlockSpec((None, C_HEADS, C_HD), lambda b: (b, 0, 0)),
        pl.BlockSpec((C_HEADS, cbuf), lambda b: (0, 0)),
        pl.BlockSpec((C_HEADS, 1), lambda b: (0, 0)),
        pl.BlockSpec((C_HEADS, 1), lambda b: (0, 0)),
    ]
    out = pl.pallas_call(
        _swa_decode_kernel,
        grid=(db,),
        in_specs=in_specs,
        out_specs=pl.BlockSpec((None, C_HEADS, C_HD), lambda b: (b, 0, 0)),
        out_shape=jax.ShapeDtypeStruct((db, C_HEADS, C_HD), F32),
        compiler_params=_cparams(1),
        name="swa_decode_attn",
    )(q64, cache_c_k, cache_c_v, kn, vn, bias, bself, sinks)
    return out.reshape(db, C_Q_W)


def _cross_prompt_kernel(q_ref, k_ref, v_ref, o_ref):
    for h in range(M_HEADS):
        sl = slice(h * HD, (h + 1) * HD)
        s = _dot_nt(q_ref[:, sl], k_ref[:, sl]) * SCALE
        m = jnp.max(s, axis=1, keepdims=True)
        e = jnp.exp(s - m)
        p = e * (1.0 / jnp.sum(e, axis=1, keepdims=True))
        o_ref[:, sl] = _dot(p.astype(_CD), v_ref[:, sl]).astype(o_ref.dtype)


def _cross_prompt_attn(q, mkv):
    b, t, _ = q.shape
    nm = mkv.shape[1]
    tq = math.gcd(t, 512)
    return pl.pallas_call(
        _cross_prompt_kernel,
        grid=(b, t // tq),
        in_specs=[pl.BlockSpec((None, tq, M_W), lambda n, i: (n, i, 0)),
                  pl.BlockSpec((None, nm, M_W), lambda n, i: (n, 0, 0)),
                  pl.BlockSpec((None, nm, M_W), lambda n, i: (n, 0, 1))],
        out_specs=pl.BlockSpec((None, tq, M_W), lambda n, i: (n, i, 0)),
        out_shape=jax.ShapeDtypeStruct((b, t, M_W), _CD),
        compiler_params=_cparams(2),
        name="cross_prompt_attn",
    )(q, mkv, mkv)


def _cross_decode_kernel(q_ref, k_ref, v_ref, o_ref):
    s = jnp.zeros((8, k_ref.shape[0]), F32)
    for h in range(M_HEADS):
        s = s + _dot_nt(q_ref[h], k_ref[:, h, :].astype(_CD))
    s = s * SCALE
    m = jnp.max(s, axis=1, keepdims=True)
    e = jnp.exp(s - m)
    p = (e * (1.0 / jnp.sum(e, axis=1, keepdims=True))).astype(_CD)
    row = lax.broadcasted_iota(jnp.int32, (8, HD), 0)
    o = jnp.zeros((8, HD), F32)
    for h in range(M_HEADS):
        o = o + jnp.where(row == h, _dot(p, v_ref[:, h, :].astype(_CD)), 0.0)
    o_ref[...] = o


def _cross_decode_attn(layer, q8, cache_m_k, cache_m_v):
    db = q8.shape[0]
    nm = cache_m_k.shape[2]
    out = pl.pallas_call(
        _cross_decode_kernel,
        grid=(db,),
        in_specs=[pl.BlockSpec((None, M_HEADS, 8, HD), lambda b: (b, 0, 0, 0)),
                  pl.BlockSpec((None, None, nm, M_HEADS, HD), lambda b: (layer, b, 0, 0, 0)),
                  pl.BlockSpec((None, None, nm, M_HEADS, HD), lambda b: (layer, b, 0, 0, 0))],
        out_specs=pl.BlockSpec((None, 8, HD), lambda b: (b, 0, 0)),
        out_shape=jax.ShapeDtypeStruct((db, 8, HD), F32),
        compiler_params=_cparams(1),
        name="cross_decode_attn",
    )(q8, cache_m_k, cache_m_v)
    return out[:, :M_HEADS, :].reshape(db, M_W)


def _spread_heads(q, n_kv, rows_per_kv, pad_rows=None):
    db, _, _, w = q.shape
    eye = jnp.eye(n_kv, dtype=q.dtype)
    out = q[:, :, None, :, :] * eye[None, :, :, None, None]
    out = out.reshape(db, n_kv, n_kv * rows_per_kv, w)
    if pad_rows is not None and pad_rows > n_kv * rows_per_kv:
        out = jnp.pad(out, ((0, 0), (0, 0), (0, pad_rows - n_kv * rows_per_kv), (0, 0)))
    return out


def kernel(x_prompt, x_sample, mem_prompt, cache_a_k, cache_a_v, cache_b_k, cache_b_v, cache_b_logf, cache_c_k, cache_c_v, cache_m_k, cache_m_v, page_table, rel_table, norm_mix, norm_cross, norm_mem, norm_ffn, w_in_even, b_forget, a_q_norm, a_k_norm, a_lambda, a_subln, b_q_norm, b_k_norm, w_out_even, w_in_odd, c_q_norm, c_k_norm, c_sinks, w_out_odd, w_mq, w_mkv, m_q_norm, m_k_norm, w_mo, w_gate, w_up, w_down):
    b, t, d = x_prompt.shape
    db = x_sample.shape[0]
    assert x_sample.shape[1] == 1
    depth = norm_mix.shape[0]
    n_mem = mem_prompt.shape[1]
    n_pages = page_table.shape[1]
    page = cache_a_k.shape[2]
    past_len = n_pages * page
    cbuf = cache_c_k.shape[2]
    d_ff = w_gate.shape[2]
    mp = b * t
    assert t % TK_A == 0 and t % WINDOW == 0 and cbuf == WINDOW and t >= cbuf

    table_a = rel_table[:, :A_HEADS]
    table_c = rel_table[:, A_HEADS:]

    rr = jnp.arange(TQ_A)[:, None]
    cc = jnp.arange(TK_A)[None, :]
    tiles, masks = [], []
    for u in range(4):
        n = u * LANES + rr - cc
        bt = _rel_bias(table_a, n)
        if u * LANES < TK_A - 1:
            bt = jnp.where((n >= 0)[None], bt, NEG)
            masks.append(jnp.where(n >= 0, 0.0, NEG).astype(F32))
        tiles.append(bt)
    assert 3 * LANES - (TK_A - 1) >= REL_MAX_DIST
    abias_p = jnp.stack(tiles, axis=1)
    fmask_p = jnp.stack(masks, axis=0)

    dist_past = past_len - jnp.arange(past_len)
    row_head = jnp.array([h * A_G + g for h in range(A_KVH) for g in range(A_G) for _ in range(2)])
    ab = _rel_bias(table_a, dist_past)[row_head]
    ab = ab.reshape(16, n_pages, page).transpose(1, 0, 2)[::-1]
    aself = jnp.broadcast_to(_rel_bias(table_a, jnp.zeros((1,), jnp.int32))[row_head], (16, LANES))

    wi = jnp.arange(WINDOW)[:, None]
    wj = jnp.arange(2 * WINDOW)[None, :]
    nband = wi + WINDOW - wj
    cbias_p = jnp.where(((nband >= 0) & (nband <= WINDOW))[None], _rel_bias(table_c, nband), NEG)
    dist_buf = cbuf - jnp.arange(cbuf)
    cbias_d = jnp.where((dist_buf <= WINDOW)[None], _rel_bias(table_c, dist_buf), NEG)
    cself = _rel_bias(table_c, jnp.zeros((1,), jnp.int32))

    xp = x_prompt.reshape(mp, d)
    xs = x_sample.reshape(db, d)
    outs = {k: [] for k in ("ak_p", "ak_s", "av_p", "av_s", "bk_p", "bk_s", "bv_p", "bv_s",
                            "lf_p", "lf_s", "ck_p", "ck_s", "cv_p", "cv_s", "mk", "mv")}
    w_down_c = w_down.astype(_CD)
    mem2 = mem_prompt.reshape(b * n_mem, d)

    for l in range(depth):
        hp = _rmsnorm(xp, norm_mix[l])
        hs = _rmsnorm(xs, norm_mix[l])
        if l % 2 == 0:
            e = l // 2
            lam_init = 0.8 - 0.6 * math.exp(-0.3 * l)
            gain = jnp.concatenate([
                jnp.tile(a_q_norm[e].reshape(-1), A_HEADS), jnp.tile(a_k_norm[e].reshape(-1), A_KVH),
                jnp.ones((A_KV_W,), F32), jnp.tile(b_q_norm[e], B_HEADS), jnp.tile(b_k_norm[e], B_KVH),
                jnp.ones((B_KV_W,), F32)])
            flag = jnp.concatenate([jnp.ones((A_Q_W + A_KV_W,)), jnp.zeros((A_KV_W,)),
                                    jnp.ones((B_Q_W + B_KV_W,)), jnp.zeros((B_KV_W,))])
            pf, pc, sf, sc = _mm(hp, hs, [w_in_even], e, EVEN_MAIN, bm=1024, bn=512, epi="norm",
                                 vec=_vec(EVEN_MAIN, gain, flag), out_dtypes=(F32, _CD))
            w_f = jnp.pad(w_in_even[e][:, EVEN_MAIN:], ((0, 0), (0, LANES - B_HEADS)))[None]
            fvec = _vec(LANES, flag=jnp.arange(LANES) < B_HEADS,
                        bias=jnp.pad(b_forget[e], (0, LANES - B_HEADS)))
            lf_p, lf_s = _mm(hp, hs, [w_f], 0, LANES, bm=1024, bn=LANES, epi="logsig", vec=fvec)

            o1, o2, o3, o4 = A_Q_W, A_Q_W + A_KV_W, A_Q_W + 2 * A_KV_W, A_Q_W + 2 * A_KV_W + B_Q_W
            o5 = o4 + B_KV_W
            outs["ak_p"].append(pf[:, o1:o2].reshape(b, t, A_KVH, 2 * HD))
            outs["av_p"].append(pf[:, o2:o3].reshape(b, t, A_KVH, 2 * HD))
            outs["bk_p"].append(pf[:, o4:o5].reshape(b, t, B_KVH, HD))
            outs["bv_p"].append(pf[:, o5:].reshape(b, t, B_KVH, HD))
            outs["lf_p"].append(lf_p[:, :B_HEADS].reshape(b, t, B_HEADS))
            outs["ak_s"].append(sf[:, o1:o2].reshape(db, 1, A_KVH, 2 * HD))
            outs["av_s"].append(sf[:, o2:o3].reshape(db, 1, A_KVH, 2 * HD))
            outs["bk_s"].append(sf[:, o4:o5].reshape(db, 1, B_KVH, HD))
            outs["bv_s"].append(sf[:, o5:].reshape(db, 1, B_KVH, HD))
            outs["lf_s"].append(lf_s[:, :B_HEADS].reshape(db, 1, B_HEADS))

            csum = _cumsum_time(lf_p.reshape(b, t, LANES))[:, :, :B_HEADS]
            cq = csum.reshape(b, t, B_KVH, B_G).transpose(0, 2, 1, 3)
            ck = csum.transpose(0, 2, 1).reshape(b, B_KVH, B_G, t // TK_A, TK_A)
            lam_p = a_lambda[e].astype(F32)
            subln = a_subln[e].reshape(1, 2 * HD).astype(F32)
            op = _even_prompt_attn(pc.reshape(b, t, EVEN_MAIN), cq, ck, abias_p, fmask_p,
                                   lam_p, subln, lam_init).reshape(mp, -1)

            qa16 = sc[:, :A_Q_W].reshape(db, A_KVH, A_G, 2, HD)
            eye2 = jnp.eye(2, dtype=_CD)
            qa16 = (qa16[:, :, :, :, None, :] * eye2[None, None, None, :, :, None]).reshape(
                db, A_KVH, A_G * 2, 2 * HD)
            qa16 = _spread_heads(qa16, A_KVH, A_G * 2)
            qb16 = _spread_heads(sc[:, o3:o4].reshape(db, B_KVH, B_G, HD), B_KVH, B_G)
            new_kv = (sc[:, o1:o2].reshape(db, A_KVH, 2 * HD), sc[:, o2:o3].reshape(db, A_KVH, 2 * HD),
                      sc[:, o4:o5].reshape(db, B_KVH, HD), sc[:, o5:].reshape(db, B_KVH, HD))
            lf_new = jnp.broadcast_to(lf_s[:, :B_HEADS, None], (db, B_HEADS, LANES))
            os_ = _even_decode_attn(e, page_table,
                                    (cache_a_k, cache_a_v, cache_b_k, cache_b_v, cache_b_logf),
                                    qa16, qb16, new_kv, lf_new, ab, aself, lam_p, subln, lam_init)
            k_perm = [(kb % 2) * A_KVH + kb // 2 for kb in range(2 * A_KVH)]
            xp, xs = _mm(op, os_.astype(_CD), [w_out_even], e, d, bm=1024, bn=512,
                         res=(xp, xs), k_perm=k_perm)
        else:
            o_ = l // 2
            gain = jnp.concatenate([jnp.tile(c_q_norm[o_], C_HEADS), jnp.tile(c_k_norm[o_], C_KVH),
                                    jnp.ones((C_KV_W,), F32)])
            flag = jnp.concatenate([jnp.ones((C_Q_W + C_KV_W,)), jnp.zeros((C_KV_W,))])
            pf, pc, sf, sc = _mm(hp, hs, [w_in_odd], o_, ODD_IN, bm=1024, bn=512, epi="norm",
                                 vec=_vec(ODD_IN, gain, flag), out_dtypes=(F32, _CD), gs=C_HD)
            kf = pf[:, C_Q_W:C_Q_W + C_KV_W].reshape(b, t, C_KVH, C_HD)
            vf = pf[:, C_Q_W + C_KV_W:].reshape(b, t, C_KVH, C_HD)
            outs["ck_p"].append(kf[:, t - cbuf:])
            outs["cv_p"].append(vf[:, t - cbuf:])
            ks_new = sf[:, C_Q_W:C_Q_W + C_KV_W].reshape(db, 1, C_KVH, C_HD)
            vs_new = sf[:, C_Q_W + C_KV_W:].reshape(db, 1, C_KVH, C_HD)
            outs["ck_s"].append(jnp.concatenate([cache_c_k[o_][:, 1:], ks_new], axis=1))
            outs["cv_s"].append(jnp.concatenate([cache_c_v[o_][:, 1:], vs_new], axis=1))

            op = _swa_prompt_attn(pc.reshape(b, t, ODD_IN), cbias_p, c_sinks[o_]).reshape(mp, C_Q_W)
            q64 = _spread_heads(sc[:, :C_Q_W].reshape(db, C_KVH, C_G, C_HD), C_KVH, C_G)
            kn = jnp.repeat(sc[:, C_Q_W:C_Q_W + C_KV_W].reshape(db, C_KVH, C_HD), C_G, axis=1)
            vn = jnp.repeat(sc[:, C_Q_W + C_KV_W:].reshape(db, C_KVH, C_HD), C_G, axis=1)
            os_ = _swa_decode_attn(o_, q64, cache_c_k, cache_c_v, kn, vn, cbias_d, cself,
                                   c_sinks[o_].reshape(C_HEADS, 1).astype(F32))
            xp, xs = _mm(op, os_.astype(_CD), [w_out_odd], o_, d, bm=1024, bn=512, res=(xp, xs))

        hm = _rmsnorm(mem2, norm_mem[l])
        mgain = jnp.concatenate([jnp.tile(m_k_norm[l], M_HEADS), jnp.ones((M_W,), F32)])
        mflag = jnp.concatenate([jnp.ones((M_W,)), jnp.zeros((M_W,))])
        mf, mc = _mm(hm, None, [w_mkv], l, 2 * M_W, bm=1024, bn=512, epi="norm",
                     vec=_vec(2 * M_W, mgain, mflag), out_dtypes=(F32, _CD))
        outs["mk"].append(mf[:, :M_W].reshape(b, n_mem, M_HEADS, HD))
        outs["mv"].append(mf[:, M_W:].reshape(b, n_mem, M_HEADS, HD))
        hp = _rmsnorm(xp, norm_cross[l])
        hs = _rmsnorm(xs, norm_cross[l])
        qvec = _vec(M_W, jnp.tile(m_q_norm[l], M_HEADS), jnp.ones((M_W,)))
        qp, qs = _mm(hp, hs, [w_mq], l, M_W, bm=1024, bn=512, epi="norm", vec=qvec,
                     out_dtypes=(_CD,))
        op = _cross_prompt_attn(qp.reshape(b, t, M_W), mc.reshape(b, n_mem, 2 * M_W)).reshape(mp, M_W)
        q8 = _spread_heads(qs.reshape(db, M_HEADS, 1, HD), M_HEADS, 1, pad_rows=8)
        os_ = _cross_decode_attn(l, q8, cache_m_k, cache_m_v)
        xp, xs = _mm(op, os_.astype(_CD), [w_mo], l, d, bm=1024, bn=512, res=(xp, xs))

        hp = _rmsnorm(xp, norm_ffn[l])
        hs = _rmsnorm(xs, norm_ffn[l])
        ap, as_ = _mm(hp, hs, [w_gate, w_up], l, d_ff, bm=1024, bn=256, epi="silu_mul",
                      out_dtypes=(_CD,))
        (xp,) = _mm(ap, None, [w_down_c], l, d, bm=512, bn=512, res=(xp,), order="xs")
        (xs,) = _mm(as_, None, [w_down_c], l, d, bm=512, bn=512, res=(xs,), order="xs")

    st = lambda k: jnp.stack(outs[k])
    return (xp.reshape(b, t, d), xs.reshape(db, 1, d),
            st("ak_p"), st("ak_s"), st("av_p"), st("av_s"), st("bk_p"), st("bk_s"),
            st("bv_p"), st("bv_s"), st("lf_p"), st("lf_s"), st("ck_p"), st("ck_s"),
            st("cv_p"), st("cv_s"), st("mk"), st("mv"))
```

```python
import functools
import math

import jax
import jax.numpy as jnp
from jax import lax
from jax.experimental import pallas as pl
from jax.experimental.pallas import tpu as pltpu

F32 = jnp.float32
_CD = jnp.bfloat16

HD = 128
A_HEADS, A_KVH = 8, 4
A_G = A_HEADS // A_KVH
B_HEADS, B_KVH = 16, 4
B_G = B_HEADS // B_KVH
C_HEADS, C_KVH, C_HD = 64, 8, 64
C_G = C_HEADS // C_KVH
WINDOW = 128
M_HEADS = 4
NUM_BUCKETS = 32
REL_MAX_DIST = 128
A_Q_W = A_HEADS * 2 * HD
A_KV_W = A_KVH * 2 * HD
B_Q_W = B_HEADS * HD
B_KV_W = B_KVH * HD
EVEN_MAIN = A_Q_W + 2 * A_KV_W + B_Q_W + 2 * B_KV_W
C_Q_W = C_HEADS * C_HD
C_KV_W = C_KVH * C_HD
ODD_IN = C_Q_W + 2 * C_KV_W
M_W = M_HEADS * HD
NEG = -1e30
M_INIT = -1e38
EPS = 1e-6
SCALE = HD ** -0.5
C_SCALE = C_HD ** -0.5

LANES = 128
VMEM_LIMIT = 60 * 1024 * 1024
TQ_A, TK_A = 128, 256
PAGES_PER_STEP = 8


def _cparams(n_axes):
    return pltpu.CompilerParams(dimension_semantics=("arbitrary",) * n_axes,
                                vmem_limit_bytes=VMEM_LIMIT)


def _dot(a, b):
    return jnp.dot(a, b, preferred_element_type=F32)


def _dot_nt(a, b):
    return lax.dot_general(a, b, (((1,), (1,)), ((), ())), preferred_element_type=F32)


def _rep(x, width):
    n = width // LANES
    return x if n == 1 else pltpu.repeat(x, n, axis=1)


def _rmsnorm_kernel(x_ref, g_ref, o_ref):
    x = x_ref[...]
    ms = jnp.mean(x * x, axis=-1, keepdims=True)
    o_ref[...] = (x * lax.rsqrt(ms + EPS) * g_ref[...]).astype(o_ref.dtype)


def _rmsnorm(x, g):
    m, d = x.shape
    bm = math.gcd(m, 256)
    return pl.pallas_call(
        _rmsnorm_kernel,
        grid=(m // bm,),
        in_specs=[pl.BlockSpec((bm, d), lambda i: (i, 0)),
                  pl.BlockSpec((1, d), lambda i: (0, 0))],
        out_specs=pl.BlockSpec((bm, d), lambda i: (i, 0)),
        out_shape=jax.ShapeDtypeStruct((m, d), _CD),
        compiler_params=_cparams(1),
        name="rmsnorm",
    )(x, g.reshape(1, d).astype(F32))


def _group_rms(x, gs):
    x2 = x * x
    if gs == LANES:
        return lax.rsqrt(jnp.mean(x2, axis=-1, keepdims=True) + EPS)
    lo = lax.broadcasted_iota(jnp.int32, x.shape, 1) < gs
    s_lo = jnp.sum(jnp.where(lo, x2, 0.0), axis=-1, keepdims=True)
    s_hi = jnp.sum(jnp.where(lo, 0.0, x2), axis=-1, keepdims=True)
    return jnp.where(lo, lax.rsqrt(s_lo / gs + EPS), lax.rsqrt(s_hi / gs + EPS))


def _epilogue(accs, vec_ref, epi, gs):
    if epi == "none":
        return accs[0]
    if epi == "silu_mul":
        g, u = accs
        return g * (1.0 / (1.0 + jnp.exp(-g))) * u
    y = accs[0]
    if epi == "logsig":
        z = y + vec_ref[2:3, :]
        ls = jnp.minimum(z, 0.0) - jnp.log(1.0 + jnp.exp(-jnp.abs(z)))
        return jnp.where(vec_ref[1:2, :] > 0, ls, 0.0)
    assert epi == "norm"
    cols = []
    for c in range(y.shape[1] // LANES):
        sl = slice(c * LANES, (c + 1) * LANES)
        x = y[:, sl]
        xn = x * _group_rms(x, gs) * vec_ref[0:1, sl]
        cols.append(jnp.where(vec_ref[1:2, sl] > 0, xn, x))
    return cols[0] if len(cols) == 1 else jnp.concatenate(cols, axis=1)


def _mm_kernel(*refs, n_w, cast_w, has_s, epi, has_res, n_out, gs, i_axis, cast_rows, k_perm):
    it = iter(refs)
    xp = next(it)
    xs = next(it) if has_s else None
    ws = [next(it) for _ in range(n_w)]
    vec = next(it) if epi in ("norm", "logsig") else None
    rp = next(it) if has_res else None
    rs = next(it) if (has_res and has_s) else None
    outs_p = [next(it) for _ in range(n_out)]
    outs_s = [next(it) for _ in range(n_out)] if has_s else []
    wbs = [next(it) for _ in range(n_w)] if cast_w else ws
    i = pl.program_id(i_axis)

    if cast_w:
        @pl.when(i == 0)
        def _():
            for w, wb in zip(ws, wbs):
                for cb in range(w.shape[0] // cast_rows):
                    src = k_perm[cb] if k_perm is not None else cb
                    wb[cb * cast_rows:(cb + 1) * cast_rows, :] = (
                        w[src * cast_rows:(src + 1) * cast_rows, :].astype(_CD))

    def emit(x_ref, r_ref, o_refs):
        y = _epilogue([_dot(x_ref[...], wb[...]) for wb in wbs], vec, epi, gs)
        if r_ref is not None:
            y = y + r_ref[...]
        for o in o_refs:
            o[...] = y.astype(o.dtype)

    emit(xp, rp, outs_p)
    if has_s:
        @pl.when(i == 0)
        def _():
            emit(xs, rs, outs_s)


def _cast_rows(k):
    for r in (512, 256, 128, 64, 32, 16):
        if k % r == 0:
            return r
    raise ValueError(k)


def _mm(xp, xs, ws, layer, n_cols, *, bm, bn, epi="none", vec=None, res=None,
        out_dtypes=(F32,), gs=LANES, k_perm=None, order="ws"):
    mp, k = xp.shape
    has_s = xs is not None
    cast_w = order == "ws"
    assert cast_w or (ws[0].dtype == _CD and not has_s)
    bm = min(bm, mp)
    bn = min(bn, n_cols)
    assert mp % bm == 0 and n_cols % bn == 0
    ni, nj = mp // bm, n_cols // bn
    if order == "ws":
        grid, i_axis = (nj, ni), 1
        ij = lambda a, b: (b, a)
    else:
        grid, i_axis = (ni, nj), 0
        ij = lambda a, b: (a, b)
    cast_rows = 512 if k_perm is not None else _cast_rows(k)

    def xmap(a, b):
        return (ij(a, b)[0], 0)

    def wmap(a, b):
        return (layer, 0, ij(a, b)[1])

    def omap(a, b):
        return ij(a, b)

    def smap(a, b):
        return (0, ij(a, b)[1])

    in_specs = [pl.BlockSpec((bm, k), xmap)]
    args = [xp]
    if has_s:
        ms = xs.shape[0]
        in_specs.append(pl.BlockSpec((ms, k), lambda a, b: (0, 0)))
        args.append(xs)
    for w in ws:
        in_specs.append(pl.BlockSpec((None, k, bn), wmap))
        args.append(w)
    if vec is not None:
        in_specs.append(pl.BlockSpec((8, bn), smap))
        args.append(vec)
    has_res = res is not None
    if has_res:
        in_specs.append(pl.BlockSpec((bm, bn), omap))
        args.append(res[0])
        if has_s:
            in_specs.append(pl.BlockSpec((ms, bn), smap))
            args.append(res[1])
    out_specs, out_shape = [], []
    for dt in out_dtypes:
        out_specs.append(pl.BlockSpec((bm, bn), omap))
        out_shape.append(jax.ShapeDtypeStruct((mp, n_cols), dt))
    if has_s:
        for dt in out_dtypes:
            out_specs.append(pl.BlockSpec((ms, bn), smap))
            out_shape.append(jax.ShapeDtypeStruct((ms, n_cols), dt))
    scratch = [pltpu.VMEM((k, bn), _CD) for _ in ws] if cast_w else []
    kern = functools.partial(
        _mm_kernel, n_w=len(ws), cast_w=cast_w, has_s=has_s, epi=epi, has_res=has_res,
        n_out=len(out_dtypes), gs=gs, i_axis=i_axis, cast_rows=cast_rows, k_perm=k_perm)
    return pl.pallas_call(
        kern, grid=grid, in_specs=in_specs, out_specs=out_specs, out_shape=out_shape,
        scratch_shapes=scratch, compiler_params=_cparams(2), name="proj_" + epi,
    )(*args)


def _vec(n, gain=None, flag=None, bias=None):
    v = jnp.zeros((8, n), F32)
    if gain is not None:
        v = v.at[0].set(gain.astype(F32))
    if flag is not None:
        v = v.at[1].set(flag.astype(F32))
    if bias is not None:
        v = v.at[2].set(bias.astype(F32))
    return v


def _rel_bucket(n):
    max_exact = NUM_BUCKETS // 2
    nf = jnp.maximum(n, 1).astype(F32)
    large = max_exact + (jnp.log(nf / max_exact) / math.log(REL_MAX_DIST / max_exact)
                         * (NUM_BUCKETS - max_exact)).astype(jnp.int32)
    large = jnp.minimum(large, NUM_BUCKETS - 1)
    return jnp.where(n < max_exact, n, large)


def _rel_bias(table, n):
    bucket = _rel_bucket(jnp.maximum(n, 0))
    onehot = (bucket[..., None] == jnp.arange(NUM_BUCKETS)).astype(F32)
    out = jnp.tensordot(onehot, table.astype(F32), axes=1, precision=lax.Precision.HIGHEST)
    return jnp.moveaxis(out, -1, 0)


def _cumsum_kernel(x_ref, o_ref, *, blk):
    t = x_ref.shape[0]
    r = lax.broadcasted_iota(jnp.int32, (blk, blk), 0)
    c = lax.broadcasted_iota(jnp.int32, (blk, blk), 1)
    tri = (c <= r).astype(F32)

    def body(b, carry):
        rows = pl.ds(pl.multiple_of(b * blk, blk), blk)
        y = jnp.dot(tri, x_ref[rows, :], preferred_element_type=F32,
                    precision=lax.Precision.HIGHEST) + carry
        o_ref[rows, :] = y
        return y[blk - 1:blk, :]

    lax.fori_loop(0, t // blk, body, jnp.zeros((1, LANES), F32))


def _cumsum_time(x):
    b, t, w = x.shape
    blk = math.gcd(t, 256)
    return pl.pallas_call(
        functools.partial(_cumsum_kernel, blk=blk),
        grid=(b,),
        in_specs=[pl.BlockSpec((None, t, w), lambda n: (n, 0, 0))],
        out_specs=pl.BlockSpec((None, t, w), lambda n: (n, 0, 0)),
        out_shape=jax.ShapeDtypeStruct((b, t, w), F32),
        compiler_params=_cparams(1),
        name="logf_cumsum",
    )(x)


def _diff_lambda(lam_ref, lam_init):
    la = lam_ref[...]
    e1 = jnp.exp(jnp.sum(la[0:1, :] * la[1:2, :], axis=-1, keepdims=True))
    e2 = jnp.exp(jnp.sum(la[2:3, :] * la[3:4, :], axis=-1, keepdims=True))
    return e1 - e2 + lam_init


def _subln(o, sub_ref, lam_init):
    ms = jnp.mean(o * o, axis=-1, keepdims=True)
    return o * lax.rsqrt(ms + EPS) * sub_ref[...] * (1.0 - lam_init)


def _online_update(s, v, m_ref, l_ref, acc_ref, sidx, aidx):
    m_prev = m_ref[sidx]
    m_new = jnp.maximum(m_prev, jnp.max(s, axis=1, keepdims=True))
    p = jnp.exp(s - _rep(m_new, s.shape[1]))
    alpha = jnp.exp(m_prev - m_new)
    l_ref[sidx] = alpha * l_ref[sidx] + jnp.sum(p, axis=1, keepdims=True)
    m_ref[sidx] = m_new
    acc_ref[aidx] = acc_ref[aidx] * _rep(alpha, v.shape[1]) + _dot(p.astype(_CD), v)


def _even_prompt_kernel(aq_ref, ak_ref, av_ref, bq_ref, bk_ref, bv_ref, cq_ref, ck_ref,
                        abias_ref, fmask_ref, lam_ref, sub_ref, o_ref,
                        m_ref, l_ref, acca_ref, accb_ref, *, lam_init):
    tq, tk = TQ_A, TK_A
    qi = pl.program_id(2)
    jd = (qi * tq) // tk
    m_ref[...] = jnp.full(m_ref.shape, M_INIT, F32)
    l_ref[...] = jnp.zeros(l_ref.shape, F32)
    acca_ref[...] = jnp.zeros(acca_ref.shape, F32)
    accb_ref[...] = jnp.zeros(accb_ref.shape, F32)

    def step(j, diag):
        ks = pl.ds(pl.multiple_of(j * tk, tk), tk)
        ka = ak_ref[ks, :]
        va = av_ref[ks, :]
        off = jnp.minimum((qi * tq - j * tk) // LANES, 3)
        for g in range(A_G):
            bias = abias_ref[g, off]
            for c in range(2):
                q = aq_ref[:, (g * 2 + c) * HD:(g * 2 + c + 1) * HD]
                s = _dot_nt(q, ka[:, c * HD:(c + 1) * HD]) * SCALE + bias
                _online_update(s, va, m_ref, l_ref, acca_ref, g * 2 + c, g * 2 + c)
        kb = bk_ref[ks, :]
        vb = bv_ref[ks, :]
        for g in range(B_G):
            q = bq_ref[:, g * HD:(g + 1) * HD]
            s = _dot_nt(q, kb) * SCALE + (cq_ref[:, g:g + 1] - ck_ref[g, pl.ds(j, 1), :])
            if diag:
                s = s + fmask_ref[off]
            _online_update(s, vb, m_ref, l_ref, accb_ref, 4 + g, g)

    def body(j, carry):
        step(j, False)
        return carry

    lax.fori_loop(0, jd, body, 0)
    step(jd, True)

    lam = _diff_lambda(lam_ref, lam_init)
    for g in range(A_G):
        o1 = acca_ref[2 * g] / _rep(l_ref[2 * g], 2 * HD)
        o2 = acca_ref[2 * g + 1] / _rep(l_ref[2 * g + 1], 2 * HD)
        o = _subln(o1 - lam * o2, sub_ref, lam_init)
        o_ref[:, g * 2 * HD:(g + 1) * 2 * HD] = o.astype(o_ref.dtype)
    for g in range(B_G):
        o = accb_ref[g] / l_ref[4 + g]
        o_ref[:, A_G * 2 * HD + g * HD:A_G * 2 * HD + (g + 1) * HD] = o.astype(o_ref.dtype)


def _even_prompt_attn(pb, cq, ck, abias, fmask, lam_p, subln, lam_init):
    b, t, _ = pb.shape
    tq, tk = TQ_A, TK_A
    nq, nkb = t // tq, t // tk
    o_w = A_G * 2 * HD + B_G * HD
    ak0 = A_Q_W // (2 * HD)
    av0 = (A_Q_W + A_KV_W) // (2 * HD)
    bq0 = (A_Q_W + 2 * A_KV_W) // (B_G * HD)
    bk0 = (A_Q_W + 2 * A_KV_W + B_Q_W) // HD
    bv0 = bk0 + B_KVH
    in_specs = [
        pl.BlockSpec((None, tq, A_G * 2 * HD), lambda n, h, q: (n, q, h)),
        pl.BlockSpec((None, t, 2 * HD), lambda n, h, q: (n, 0, ak0 + h)),
        pl.BlockSpec((None, t, 2 * HD), lambda n, h, q: (n, 0, av0 + h)),
        pl.BlockSpec((None, tq, B_G * HD), lambda n, h, q: (n, q, bq0 + h)),
        pl.BlockSpec((None, t, HD), lambda n, h, q: (n, 0, bk0 + h)),
        pl.BlockSpec((None, t, HD), lambda n, h, q: (n, 0, bv0 + h)),
        pl.BlockSpec((None, None, tq, B_G), lambda n, h, q: (n, h, q, 0)),
        pl.BlockSpec((None, None, B_G, nkb, tk), lambda n, h, q: (n, h, 0, 0, 0)),
        pl.BlockSpec((A_G, 4, tq, tk), lambda n, h, q: (h, 0, 0, 0)),
        pl.BlockSpec((2, tq, tk), lambda n, h, q: (0, 0, 0)),
        pl.BlockSpec((4, HD), lambda n, h, q: (0, 0)),
        pl.BlockSpec((1, 2 * HD), lambda n, h, q: (0, 0)),
    ]
    return pl.pallas_call(
        functools.partial(_even_prompt_kernel, lam_init=lam_init),
        grid=(b, A_KVH, nq),
        in_specs=in_specs,
        out_specs=pl.BlockSpec((None, tq, o_w), lambda n, h, q: (n, q, h)),
        out_shape=jax.ShapeDtypeStruct((b, t, A_KVH * o_w), _CD),
        scratch_shapes=[pltpu.VMEM((8, tq, LANES), F32), pltpu.VMEM((8, tq, LANES), F32),
                        pltpu.VMEM((4, tq, 2 * HD), F32), pltpu.VMEM((4, tq, HD), F32)],
        compiler_params=_cparams(3),
        name="even_prompt_attn",
    )(pb, pb, pb, pb, pb, pb, cq, ck, abias, fmask, lam_p, subln)


def _even_decode_kernel(pt_ref, qa_ref, qb_ref, kan_ref, van_ref, kbn_ref, vbn_ref, lfn_ref,
                        abias_ref, bmask_ref, later_ref, aself_ref, lam_ref, sub_ref, *rest,
                        n_chunks, lam_init):
    del pt_ref
    npg = PAGES_PER_STEP
    pages = rest[:5 * npg]
    o_ref = rest[5 * npg]
    m_ref, l_ref, acca_ref, accb_ref, run_ref = rest[5 * npg + 1:]
    c = pl.program_id(1)

    @pl.when(c == 0)
    def _():
        sa = jnp.sum(qa_ref[...].astype(F32) * kan_ref[...].astype(F32), axis=1, keepdims=True)
        sb = jnp.sum(qb_ref[...].astype(F32) * kbn_ref[...].astype(F32), axis=1, keepdims=True)
        m_ref[0] = sa * SCALE + aself_ref[...]
        m_ref[1] = jnp.broadcast_to(sb * SCALE, (16, LANES))
        l_ref[...] = jnp.ones(l_ref.shape, F32)
        acca_ref[...] = van_ref[...].astype(F32)
        accb_ref[...] = vbn_ref[...].astype(F32)
        run_ref[...] = lfn_ref[...]

    page = pages[0].shape[0]
    cols = page * A_KVH
    tdims = (((0,), (0,)), ((), ()))

    def update(idx, s, v, acc_ref):
        m_prev = m_ref[idx]
        m_new = jnp.maximum(m_prev, jnp.max(s, axis=1, keepdims=True))
        pr = jnp.exp(s - _rep(m_new, s.shape[1]))
        alpha = jnp.exp(m_prev - m_new)
        l_ref[idx] = alpha * l_ref[idx] + jnp.sum(pr, axis=1, keepdims=True)
        m_ref[idx] = m_new
        acc_ref[...] = acc_ref[...] * _rep(alpha, v.shape[1]) + _dot(pr.astype(_CD), v)

    for p in range(npg):
        ka_ref, va_ref, kb_ref, vb_ref, lf_ref = pages[5 * p:5 * p + 5]
        lf = lf_ref[...]
        hi = lf.astype(_CD)
        r1 = lf - hi.astype(F32)
        mid = r1.astype(_CD)
        lo = (r1 - mid.astype(F32)).astype(_CD)
        sums = lax.dot_general(jnp.concatenate([hi, mid, lo], axis=1), later_ref[...], tdims,
                               preferred_element_type=F32)
        sums = sums[0:16] + sums[16:32] + sums[32:48]
        run = run_ref[...]
        fbias = _rep(run, cols) + sums[:, :cols] + bmask_ref[...]
        run_ref[...] = run + sums[:, cols:]
        xa = ka_ref[...].reshape(cols, 2 * HD).astype(_CD)
        sa = _dot_nt(qa_ref[...], xa) * SCALE + abias_ref[p]
        update(0, sa, va_ref[...].reshape(cols, 2 * HD).astype(_CD), acca_ref)
        xb = kb_ref[...].reshape(cols, HD).astype(_CD)
        sb = _dot_nt(qb_ref[...], xb) * SCALE + fbias
        update(1, sb, vb_ref[...].reshape(cols, HD).astype(_CD), accb_ref)

    @pl.when(c == n_chunks - 1)
    def _():
        lam = _diff_lambda(lam_ref, lam_init)
        o_w = A_G * 2 * HD + B_G * HD
        la = l_ref[0]
        lb = l_ref[1]
        for h in range(A_KVH):
            for g in range(A_G):
                r1 = h * 4 + g * 2
                o1 = acca_ref[r1:r1 + 1, :] / _rep(la[r1:r1 + 1, :], 2 * HD)
                o2 = acca_ref[r1 + 1:r1 + 2, :] / _rep(la[r1 + 1:r1 + 2, :], 2 * HD)
                o = _subln(o1 - lam * o2, sub_ref, lam_init)
                o_ref[0:1, h * o_w + g * 2 * HD:h * o_w + (g + 1) * 2 * HD] = o
            for g in range(B_G):
                rr = h * 4 + g
                o = accb_ref[rr:rr + 1, :] / lb[rr:rr + 1, :]
                o_ref[0:1, h * o_w + A_G * 2 * HD + g * HD:h * o_w + A_G * 2 * HD + (g + 1) * HD] = o


def _even_decode_attn(e, page_table, caches, qa16, qb16, new_kv, lf_new, abias, bmask, later,
                      aself, lam_p, subln, lam_init):
    cache_a_k, cache_a_v, cache_b_k, cache_b_v, cache_b_logf = caches
    db, n_pages = page_table.shape
    page = cache_a_k.shape[2]
    npg = PAGES_PER_STEP
    assert n_pages % npg == 0
    n_chunks = n_pages // npg
    kan, van, kbn, vbn = new_kv
    cols = page * A_KVH

    def fixed(shape):
        nd = len(shape)
        return pl.BlockSpec((None,) + shape, lambda b, c, pt: (b,) + (0,) * nd)

    def const(shape):
        nd = len(shape)
        return pl.BlockSpec(shape, lambda b, c, pt: (0,) * nd)

    in_specs = [
        fixed((16, 2 * HD)), fixed((16, HD)),
        fixed((16, 2 * HD)), fixed((16, 2 * HD)), fixed((16, HD)), fixed((16, HD)),
        fixed((16, LANES)),
        pl.BlockSpec((npg, 16, cols), lambda b, c, pt: (c, 0, 0)),
        const((16, cols)), const((page, cols + LANES)),
        const((16, LANES)), const((4, HD)), const((1, 2 * HD)),
    ]
    args = [qa16, qb16, kan, van, kbn, vbn, lf_new, abias, bmask, later, aself, lam_p, subln]
    for p in range(npg):
        def pmap(b, c, pt, p=p):
            return (e, pt[b, n_pages - 1 - (c * npg + p)], 0, 0, 0)

        def fmap(b, c, pt, p=p):
            return (e, pt[b, n_pages - 1 - (c * npg + p)], 0, 0)

        in_specs += [
            pl.BlockSpec((None, None, page, A_KVH, 2 * HD), pmap),
            pl.BlockSpec((None, None, page, A_KVH, 2 * HD), pmap),
            pl.BlockSpec((None, None, page, B_KVH, HD), pmap),
            pl.BlockSpec((None, None, page, B_KVH, HD), pmap),
            pl.BlockSpec((None, None, page, B_HEADS), fmap),
        ]
        args += [cache_a_k, cache_a_v, cache_b_k, cache_b_v, cache_b_logf]
    o_w = A_KVH * (A_G * 2 * HD + B_G * HD)
    out = pl.pallas_call(
        functools.partial(_even_decode_kernel, n_chunks=n_chunks, lam_init=lam_init),
        grid_spec=pltpu.PrefetchScalarGridSpec(
            num_scalar_prefetch=1,
            grid=(db, n_chunks),
            in_specs=in_specs,
            out_specs=pl.BlockSpec((None, 1, o_w), lambda b, c, pt: (b, 0, 0)),
            scratch_shapes=[pltpu.VMEM((2, 16, LANES), F32), pltpu.VMEM((2, 16, LANES), F32),
                            pltpu.VMEM((16, 2 * HD), F32), pltpu.VMEM((16, HD), F32),
                            pltpu.VMEM((16, LANES), F32)],
        ),
        out_shape=jax.ShapeDtypeStruct((db, 1, o_w), F32),
        compiler_params=_cparams(2),
        name="even_decode_attn",
    )(page_table, *args)
    return out.reshape(db, o_w)


def _swa_prompt_kernel(sink_ref, q_ref, kp_ref, kc_ref, vp_ref, vc_ref, bias_ref, o_ref):
    qi = pl.program_id(1)
    kp = pl.program_id(2)
    w = WINDOW
    k_cat = jnp.concatenate([kp_ref[...], kc_ref[...]], axis=0)
    v_cat = jnp.concatenate([vp_ref[...], vc_ref[...]], axis=0)
    col = lax.broadcasted_iota(jnp.int32, (w, 2 * w), 1)
    keep = (col >= w) | (qi > 0)
    outs = []
    for kv in range(2):
        k_h = k_cat[:, kv * C_HD:(kv + 1) * C_HD]
        v_h = v_cat[:, kv * C_HD:(kv + 1) * C_HD]
        for g in range(C_G):
            hh = kv * C_G + g
            q = q_ref[:, hh * C_HD:(hh + 1) * C_HD]
            s = _dot_nt(q, k_h) + jnp.where(keep, bias_ref[hh], NEG)
            sink = sink_ref[kp * 2 * C_G + hh]
            m = jnp.maximum(jnp.max(jnp.maximum(s[:, :w], s[:, w:]), axis=1, keepdims=True), sink)
            e = jnp.exp(s - m)
            den = jnp.sum(e[:, :w] + e[:, w:], axis=1, keepdims=True) + jnp.exp(sink - m)
            outs.append(_dot((e * (1.0 / den)).astype(_CD), v_h))
    o_ref[...] = jnp.concatenate(outs, axis=1).astype(o_ref.dtype)


def _swa_prompt_attn(pb, bias, sinks):
    b, t, _ = pb.shape
    w = WINDOW
    nq = t // w
    qw = 2 * C_G * C_HD
    k0 = C_Q_W // LANES
    v0 = (C_Q_W + C_KV_W) // LANES
    prev = lambda q: jnp.maximum(q - 1, 0)
    in_specs = [
        pl.BlockSpec(memory_space=pltpu.SMEM),
        pl.BlockSpec((None, w, qw), lambda n, q, k: (n, q, k)),
        pl.BlockSpec((None, w, LANES), lambda n, q, k: (n, prev(q), k0 + k)),
        pl.BlockSpec((None, w, LANES), lambda n, q, k: (n, q, k0 + k)),
        pl.BlockSpec((None, w, LANES), lambda n, q, k: (n, prev(q), v0 + k)),
        pl.BlockSpec((None, w, LANES), lambda n, q, k: (n, q, v0 + k)),
        pl.BlockSpec((2 * C_G, w, 2 * w), lambda n, q, k: (k, 0, 0)),
    ]
    return pl.pallas_call(
        _swa_prompt_kernel,
        grid=(b, nq, C_KVH // 2),
        in_specs=in_specs,
        out_specs=pl.BlockSpec((None, w, qw), lambda n, q, k: (n, q, k)),
        out_shape=jax.ShapeDtypeStruct((b, t, C_Q_W), _CD),
        compiler_params=_cparams(3),
        name="swa_prompt_attn",
    )(sinks.astype(F32), pb, pb, pb, pb, pb, bias)


def _swa_decode_kernel(q_ref, kbuf_ref, vbuf_ref, kn_ref, vn_ref, bias_ref, bself_ref, sink_ref,
                       o_ref):
    cols = kbuf_ref.shape[0] * C_KVH
    q = q_ref[...]
    s = _dot_nt(q, kbuf_ref[...].reshape(cols, C_HD).astype(_CD)) + bias_ref[...]
    s_new = (jnp.sum(q.astype(F32) * kn_ref[...].astype(F32), axis=1, keepdims=True)
             + bself_ref[...])
    sink = sink_ref[...]
    m = jnp.maximum(jnp.maximum(jnp.max(s, axis=1, keepdims=True), s_new), sink)
    e = jnp.exp(s - m)
    e_new = jnp.exp(s_new - m)
    den = jnp.sum(e, axis=1, keepdims=True) + e_new + jnp.exp(sink - m)
    inv = 1.0 / den
    o = (e_new * inv).astype(_CD).astype(F32) * vn_ref[...].astype(F32)
    o_ref[...] = o + _dot((e * inv).astype(_CD), vbuf_ref[...].reshape(cols, C_HD).astype(_CD))


def _swa_decode_attn(o_idx, q64, cache_c_k, cache_c_v, kn, vn, bias, bself, sinks):
    db = q64.shape[0]
    cbuf = cache_c_k.shape[2]
    in_specs = [
        pl.BlockSpec((None, C_HEADS, C_HD), lambda b: (b, 0, 0)),
        pl.BlockSpec((None, None, cbuf, C_KVH, C_HD), lambda b: (o_idx, b, 0, 0, 0)),
        pl.BlockSpec((None, None, cbuf, C_KVH, C_HD), lambda b: (o_idx, b, 0, 0, 0)),
        pl.BlockSpec((None, C_HEADS, C_HD), lambda b: (b, 0, 0)),
        pl.BlockSpec((None, C_HEADS, C_HD), lambda b: (b, 0, 0)),
        pl.BlockSpec((C_HEADS, cbuf * C_KVH), lambda b: (0, 0)),
        pl.BlockSpec((C_HEADS, 1), lambda b: (0, 0)),
        pl.BlockSpec((C_HEADS, 1), lambda b: (0, 0)),
    ]
    out = pl.pallas_call(
        _swa_decode_kernel,
        grid=(db,),
        in_specs=in_specs,
        out_specs=pl.BlockSpec((None, C_HEADS, C_HD), lambda b: (b, 0, 0)),
        out_shape=jax.ShapeDtypeStruct((db, C_HEADS, C_HD), F32),
        compiler_params=_cparams(1),
        name="swa_decode_attn",
    )(q64, cache_c_k, cache_c_v, kn, vn, bias, bself, sinks)
    return out.reshape(db, C_Q_W)


def _cross_prompt_kernel(q_ref, k_ref, v_ref, o_ref):
    for h in range(M_HEADS):
        sl = slice(h * HD, (h + 1) * HD)
        s = _dot_nt(q_ref[:, sl], k_ref[:, sl]) * SCALE
        m = jnp.max(s, axis=1, keepdims=True)
        e = jnp.exp(s - m)
        p = e * (1.0 / jnp.sum(e, axis=1, keepdims=True))
        o_ref[:, sl] = _dot(p.astype(_CD), v_ref[:, sl]).astype(o_ref.dtype)


def _cross_prompt_attn(q, mkv):
    b, t, _ = q.shape
    nm = mkv.shape[1]
    tq = math.gcd(t, 512)
    return pl.pallas_call(
        _cross_prompt_kernel,
        grid=(b, t // tq),
        in_specs=[pl.BlockSpec((None, tq, M_W), lambda n, i: (n, i, 0)),
                  pl.BlockSpec((None, nm, M_W), lambda n, i: (n, 0, 0)),
                  pl.BlockSpec((None, nm, M_W), lambda n, i: (n, 0, 1))],
        out_specs=pl.BlockSpec((None, tq, M_W), lambda n, i: (n, i, 0)),
        out_shape=jax.ShapeDtypeStruct((b, t, M_W), _CD),
        compiler_params=_cparams(2),
        name="cross_prompt_attn",
    )(q, mkv, mkv)


def _cross_decode_kernel(q_ref, k_ref, v_ref, mask_ref, o_ref):
    cols = k_ref.shape[0] * M_HEADS
    s = _dot_nt(q_ref[...], k_ref[...].reshape(cols, HD).astype(_CD)) * SCALE + mask_ref[...]
    m = jnp.max(s, axis=1, keepdims=True)
    e = jnp.exp(s - m)
    p = (e * (1.0 / jnp.sum(e, axis=1, keepdims=True))).astype(_CD)
    o_ref[...] = _dot(p, v_ref[...].reshape(cols, HD).astype(_CD))


def _cross_decode_attn(layer, q8, cache_m_k, cache_m_v):
    db = q8.shape[0]
    nm = cache_m_k.shape[2]
    cols = nm * M_HEADS
    mask = jnp.where((jnp.arange(8) % M_HEADS)[:, None] == (jnp.arange(cols) % M_HEADS)[None, :],
                     0.0, NEG).astype(F32)
    out = pl.pallas_call(
        _cross_decode_kernel,
        grid=(db,),
        in_specs=[pl.BlockSpec((None, 8, HD), lambda b: (b, 0, 0)),
                  pl.BlockSpec((None, None, nm, M_HEADS, HD), lambda b: (layer, b, 0, 0, 0)),
                  pl.BlockSpec((None, None, nm, M_HEADS, HD), lambda b: (layer, b, 0, 0, 0)),
                  pl.BlockSpec((8, cols), lambda b: (0, 0))],
        out_specs=pl.BlockSpec((None, 8, HD), lambda b: (b, 0, 0)),
        out_shape=jax.ShapeDtypeStruct((db, 8, HD), F32),
        compiler_params=_cparams(1),
        name="cross_decode_attn",
    )(q8, cache_m_k, cache_m_v, mask)
    return out[:, :M_HEADS, :].reshape(db, M_W)


def kernel(x_prompt, x_sample, mem_prompt, cache_a_k, cache_a_v, cache_b_k, cache_b_v, cache_b_logf, cache_c_k, cache_c_v, cache_m_k, cache_m_v, page_table, rel_table, norm_mix, norm_cross, norm_mem, norm_ffn, w_in_even, b_forget, a_q_norm, a_k_norm, a_lambda, a_subln, b_q_norm, b_k_norm, w_out_even, w_in_odd, c_q_norm, c_k_norm, c_sinks, w_out_odd, w_mq, w_mkv, m_q_norm, m_k_norm, w_mo, w_gate, w_up, w_down):
    b, t, d = x_prompt.shape
    db = x_sample.shape[0]
    assert x_sample.shape[1] == 1
    depth = norm_mix.shape[0]
    n_mem = mem_prompt.shape[1]
    n_pages = page_table.shape[1]
    page = cache_a_k.shape[2]
    past_len = n_pages * page
    cbuf = cache_c_k.shape[2]
    d_ff = w_gate.shape[2]
    mp = b * t
    assert t % TK_A == 0 and t % WINDOW == 0 and cbuf == WINDOW and t >= cbuf

    table_a = rel_table[:, :A_HEADS]
    table_c = rel_table[:, A_HEADS:]

    rr = jnp.arange(TQ_A)[:, None]
    cc = jnp.arange(TK_A)[None, :]
    tiles, masks = [], []
    for u in range(4):
        n = u * LANES + rr - cc
        bt = _rel_bias(table_a, n)
        if u * LANES < TK_A - 1:
            bt = jnp.where((n >= 0)[None], bt, NEG)
            masks.append(jnp.where(n >= 0, 0.0, NEG).astype(F32))
        tiles.append(bt)
    assert 3 * LANES - (TK_A - 1) >= REL_MAX_DIST
    abias_p = jnp.stack(tiles, axis=1)
    fmask_p = jnp.stack(masks, axis=0)

    dist_past = past_len - jnp.arange(past_len)
    row_head = jnp.array([h * A_G + g for h in range(A_KVH) for g in range(A_G) for _ in range(2)])
    ab = _rel_bias(table_a, dist_past)[row_head]
    ab = ab.reshape(16, n_pages, page).transpose(1, 0, 2)[::-1]
    cols = page * A_KVH
    head_mask = jnp.where((jnp.arange(16) // 4)[:, None] == (jnp.arange(cols) % A_KVH)[None, :],
                          0.0, NEG).astype(F32)
    ab = jnp.repeat(ab, A_KVH, axis=2) + head_mask[None]
    later = jnp.concatenate(
        [(jnp.arange(page)[:, None] > (jnp.arange(cols) // A_KVH)[None, :]).astype(_CD),
         jnp.ones((page, LANES), _CD)], axis=1)
    aself = jnp.broadcast_to(_rel_bias(table_a, jnp.zeros((1,), jnp.int32))[row_head], (16, LANES))

    wi = jnp.arange(WINDOW)[:, None]
    wj = jnp.arange(2 * WINDOW)[None, :]
    nband = wi + WINDOW - wj
    cbias_p = jnp.where(((nband >= 0) & (nband <= WINDOW))[None], _rel_bias(table_c, nband), NEG)
    dist_buf = cbuf - jnp.arange(cbuf)
    cbias_d = jnp.where((dist_buf <= WINDOW)[None], _rel_bias(table_c, dist_buf), NEG)
    ccols = cbuf * C_KVH
    cbias_d = jnp.repeat(cbias_d, C_KVH, axis=1) + jnp.where(
        (jnp.arange(C_HEADS) // C_G)[:, None] == (jnp.arange(ccols) % C_KVH)[None, :], 0.0, NEG)
    cself = _rel_bias(table_c, jnp.zeros((1,), jnp.int32))

    xp = x_prompt.reshape(mp, d)
    xs = x_sample.reshape(db, d)
    outs = {k: [] for k in ("ak_p", "ak_s", "av_p", "av_s", "bk_p", "bk_s", "bv_p", "bv_s",
                            "lf_p", "lf_s", "ck_p", "ck_s", "cv_p", "cv_s", "mk", "mv")}
    w_down_c = w_down.astype(_CD)
    mem2 = mem_prompt.reshape(b * n_mem, d)

    for l in range(depth):
        hp = _rmsnorm(xp, norm_mix[l])
        hs = _rmsnorm(xs, norm_mix[l])
        if l % 2 == 0:
            e = l // 2
            lam_init = 0.8 - 0.6 * math.exp(-0.3 * l)
            gain = jnp.concatenate([
                jnp.tile(a_q_norm[e].reshape(-1), A_HEADS), jnp.tile(a_k_norm[e].reshape(-1), A_KVH),
                jnp.ones((A_KV_W,), F32), jnp.tile(b_q_norm[e], B_HEADS), jnp.tile(b_k_norm[e], B_KVH),
                jnp.ones((B_KV_W,), F32)])
            flag = jnp.concatenate([jnp.ones((A_Q_W + A_KV_W,)), jnp.zeros((A_KV_W,)),
                                    jnp.ones((B_Q_W + B_KV_W,)), jnp.zeros((B_KV_W,))])
            pf, pc, sf, sc = _mm(hp, hs, [w_in_even], e, EVEN_MAIN, bm=1024, bn=512, epi="norm",
                                 vec=_vec(EVEN_MAIN, gain, flag), out_dtypes=(F32, _CD))
            w_f = jnp.pad(w_in_even[e][:, EVEN_MAIN:], ((0, 0), (0, LANES - B_HEADS)))[None]
            fvec = _vec(LANES, flag=jnp.arange(LANES) < B_HEADS,
                        bias=jnp.pad(b_forget[e], (0, LANES - B_HEADS)))
            lf_p, lf_s = _mm(hp, hs, [w_f], 0, LANES, bm=1024, bn=LANES, epi="logsig", vec=fvec)

            o1, o2, o3, o4 = A_Q_W, A_Q_W + A_KV_W, A_Q_W + 2 * A_KV_W, A_Q_W + 2 * A_KV_W + B_Q_W
            o5 = o4 + B_KV_W
            outs["ak_p"].append(pf[:, o1:o2].reshape(b, t, A_KVH, 2 * HD))
            outs["av_p"].append(pf[:, o2:o3].reshape(b, t, A_KVH, 2 * HD))
            outs["bk_p"].append(pf[:, o4:o5].reshape(b, t, B_KVH, HD))
            outs["bv_p"].append(pf[:, o5:].reshape(b, t, B_KVH, HD))
            outs["lf_p"].append(lf_p[:, :B_HEADS].reshape(b, t, B_HEADS))
            outs["ak_s"].append(sf[:, o1:o2].reshape(db, 1, A_KVH, 2 * HD))
            outs["av_s"].append(sf[:, o2:o3].reshape(db, 1, A_KVH, 2 * HD))
            outs["bk_s"].append(sf[:, o4:o5].reshape(db, 1, B_KVH, HD))
            outs["bv_s"].append(sf[:, o5:].reshape(db, 1, B_KVH, HD))
            outs["lf_s"].append(lf_s[:, :B_HEADS].reshape(db, 1, B_HEADS))

            csum = _cumsum_time(lf_p.reshape(b, t, LANES))[:, :, :B_HEADS]
            cq = csum.reshape(b, t, B_KVH, B_G).transpose(0, 2, 1, 3)
            ck = csum.transpose(0, 2, 1).reshape(b, B_KVH, B_G, t // TK_A, TK_A)
            lam_p = a_lambda[e].astype(F32)
            subln = a_subln[e].reshape(1, 2 * HD).astype(F32)
            op = _even_prompt_attn(pc.reshape(b, t, EVEN_MAIN), cq, ck, abias_p, fmask_p,
                                   lam_p, subln, lam_init).reshape(mp, -1)

            qa16 = sc[:, :A_Q_W].reshape(db, A_KVH, A_G, 2, HD)
            eye2 = jnp.eye(2, dtype=_CD)
            qa16 = (qa16[:, :, :, :, None, :] * eye2[None, None, None, :, :, None]).reshape(
                db, 16, 2 * HD)
            qb16 = sc[:, o3:o4].reshape(db, 16, HD)
            per_row = lambda x, w: jnp.repeat(x.reshape(db, A_KVH, w), 4, axis=1)
            new_kv = (per_row(sc[:, o1:o2], 2 * HD), per_row(sc[:, o2:o3], 2 * HD),
                      per_row(sc[:, o4:o5], HD), per_row(sc[:, o5:], HD))
            lf_new = jnp.broadcast_to(lf_s[:, :B_HEADS, None], (db, B_HEADS, LANES))
            os_ = _even_decode_attn(e, page_table,
                                    (cache_a_k, cache_a_v, cache_b_k, cache_b_v, cache_b_logf),
                                    qa16, qb16, new_kv, lf_new, ab, head_mask, later, aself,
                                    lam_p, subln, lam_init)
            k_perm = [(kb % 2) * A_KVH + kb // 2 for kb in range(2 * A_KVH)]
            xp, xs = _mm(op, os_.astype(_CD), [w_out_even], e, d, bm=1024, bn=512,
                         res=(xp, xs), k_perm=k_perm)
        else:
            o_ = l // 2
            gain = jnp.concatenate([jnp.tile(c_q_norm[o_] * C_SCALE, C_HEADS),
                                    jnp.tile(c_k_norm[o_], C_KVH), jnp.ones((C_KV_W,), F32)])
            flag = jnp.concatenate([jnp.ones((C_Q_W + C_KV_W,)), jnp.zeros((C_KV_W,))])
            pf, pc, sf, sc = _mm(hp, hs, [w_in_odd], o_, ODD_IN, bm=1024, bn=512, epi="norm",
                                 vec=_vec(ODD_IN, gain, flag), out_dtypes=(F32, _CD), gs=C_HD)
            kf = pf[:, C_Q_W:C_Q_W + C_KV_W].reshape(b, t, C_KVH, C_HD)
            vf = pf[:, C_Q_W + C_KV_W:].reshape(b, t, C_KVH, C_HD)
            outs["ck_p"].append(kf[:, t - cbuf:])
            outs["cv_p"].append(vf[:, t - cbuf:])
            ks_new = sf[:, C_Q_W:C_Q_W + C_KV_W].reshape(db, 1, C_KVH, C_HD)
            vs_new = sf[:, C_Q_W + C_KV_W:].reshape(db, 1, C_KVH, C_HD)
            outs["ck_s"].append(jnp.concatenate([cache_c_k[o_][:, 1:], ks_new], axis=1))
            outs["cv_s"].append(jnp.concatenate([cache_c_v[o_][:, 1:], vs_new], axis=1))

            op = _swa_prompt_attn(pc.reshape(b, t, ODD_IN), cbias_p, c_sinks[o_]).reshape(mp, C_Q_W)
            q64 = sc[:, :C_Q_W].reshape(db, C_HEADS, C_HD)
            kn = jnp.repeat(sc[:, C_Q_W:C_Q_W + C_KV_W].reshape(db, C_KVH, C_HD), C_G, axis=1)
            vn = jnp.repeat(sc[:, C_Q_W + C_KV_W:].reshape(db, C_KVH, C_HD), C_G, axis=1)
            os_ = _swa_decode_attn(o_, q64, cache_c_k, cache_c_v, kn, vn, cbias_d, cself,
                                   c_sinks[o_].reshape(C_HEADS, 1).astype(F32))
            xp, xs = _mm(op, os_.astype(_CD), [w_out_odd], o_, d, bm=1024, bn=512, res=(xp, xs))

        hm = _rmsnorm(mem2, norm_mem[l])
        mgain = jnp.concatenate([jnp.tile(m_k_norm[l], M_HEADS), jnp.ones((M_W,), F32)])
        mflag = jnp.concatenate([jnp.ones((M_W,)), jnp.zeros((M_W,))])
        mf, mc = _mm(hm, None, [w_mkv], l, 2 * M_W, bm=1024, bn=512, epi="norm",
                     vec=_vec(2 * M_W, mgain, mflag), out_dtypes=(F32, _CD))
        outs["mk"].append(mf[:, :M_W].reshape(b, n_mem, M_HEADS, HD))
        outs["mv"].append(mf[:, M_W:].reshape(b, n_mem, M_HEADS, HD))
        hp = _rmsnorm(xp, norm_cross[l])
        hs = _rmsnorm(xs, norm_cross[l])
        qvec = _vec(M_W, jnp.tile(m_q_norm[l], M_HEADS), jnp.ones((M_W,)))
        qp, qs = _mm(hp, hs, [w_mq], l, M_W, bm=1024, bn=512, epi="norm", vec=qvec,
                     out_dtypes=(_CD,))
        op = _cross_prompt_attn(qp.reshape(b, t, M_W), mc.reshape(b, n_mem, 2 * M_W)).reshape(mp, M_W)
        q8 = jnp.pad(qs.reshape(db, M_HEADS, HD), ((0, 0), (0, 8 - M_HEADS), (0, 0)))
        os_ = _cross_decode_attn(l, q8, cache_m_k, cache_m_v)
        xp, xs = _mm(op, os_.astype(_CD), [w_mo], l, d, bm=1024, bn=512, res=(xp, xs))

        hp = _rmsnorm(xp, norm_ffn[l])
        hs = _rmsnorm(xs, norm_ffn[l])
        ap, as_ = _mm(hp, hs, [w_gate, w_up], l, d_ff, bm=1024, bn=256, epi="silu_mul",
                      out_dtypes=(_CD,))
        (xp,) = _mm(ap, None, [w_down_c], l, d, bm=512, bn=512, res=(xp,), order="xs")
        (xs,) = _mm(as_, None, [w_down_c], l, d, bm=512, bn=512, res=(xs,), order="xs")

    st = lambda k: jnp.stack(outs[k])
    return (xp.reshape(b, t, d), xs.reshape(db, 1, d),
            st("ak_p"), st("ak_s"), st("av_p"), st("av_s"), st("bk_p"), st("bk_s"),
            st("bv_p"), st("bv_s"), st("lf_p"), st("lf_s"), st("ck_p"), st("ck_s"),
            st("cv_p"), st("cv_s"), st("mk"), st("mv"))
```

```python
import functools
import math

import jax
import jax.numpy as jnp
from jax import lax
from jax.experimental import pallas as pl
from jax.experimental.pallas import tpu as pltpu

F32 = jnp.float32
_CD = jnp.bfloat16

HD = 128
A_HEADS, A_KVH = 8, 4
A_G = A_HEADS // A_KVH
B_HEADS, B_KVH = 16, 4
B_G = B_HEADS // B_KVH
C_HEADS, C_KVH, C_HD = 64, 8, 64
C_G = C_HEADS // C_KVH
WINDOW = 128
M_HEADS = 4
NUM_BUCKETS = 32
REL_MAX_DIST = 128
A_Q_W = A_HEADS * 2 * HD
A_KV_W = A_KVH * 2 * HD
B_Q_W = B_HEADS * HD
B_KV_W = B_KVH * HD
EVEN_MAIN = A_Q_W + 2 * A_KV_W + B_Q_W + 2 * B_KV_W
C_Q_W = C_HEADS * C_HD
C_KV_W = C_KVH * C_HD
ODD_IN = C_Q_W + 2 * C_KV_W
M_W = M_HEADS * HD
NEG = -1e30
M_INIT = -1e38
EPS = 1e-6
SCALE = HD ** -0.5
C_SCALE = C_HD ** -0.5

LANES = 128
VMEM_LIMIT = 60 * 1024 * 1024
TQ_A, TK_A = 256, 256
PAGES_PER_STEP = 8


def _cparams(n_axes):
    return pltpu.CompilerParams(dimension_semantics=("arbitrary",) * n_axes,
                                vmem_limit_bytes=VMEM_LIMIT)


def _dot(a, b):
    return jnp.dot(a, b, preferred_element_type=F32)


def _dot_nt(a, b):
    return lax.dot_general(a, b, (((1,), (1,)), ((), ())), preferred_element_type=F32)


def _rep(x, width):
    n = width // LANES
    return x if n == 1 else jnp.concatenate([x] * n, axis=1)


def _rmsnorm_kernel(x_ref, g_ref, o_ref):
    x = x_ref[...]
    ms = jnp.mean(x * x, axis=-1, keepdims=True)
    o_ref[...] = (x * lax.rsqrt(ms + EPS) * g_ref[...]).astype(o_ref.dtype)


def _rmsnorm(x, g):
    m, d = x.shape
    bm = math.gcd(m, 256)
    return pl.pallas_call(
        _rmsnorm_kernel,
        grid=(m // bm,),
        in_specs=[pl.BlockSpec((bm, d), lambda i: (i, 0)),
                  pl.BlockSpec((1, d), lambda i: (0, 0))],
        out_specs=pl.BlockSpec((bm, d), lambda i: (i, 0)),
        out_shape=jax.ShapeDtypeStruct((m, d), _CD),
        compiler_params=_cparams(1),
        name="rmsnorm",
    )(x, g.reshape(1, d).astype(F32))


def _group_rms(x, gs):
    x2 = x * x
    if gs == LANES:
        return lax.rsqrt(jnp.mean(x2, axis=-1, keepdims=True) + EPS)
    lo = lax.broadcasted_iota(jnp.int32, x.shape, 1) < gs
    s_lo = jnp.sum(jnp.where(lo, x2, 0.0), axis=-1, keepdims=True)
    s_hi = jnp.sum(jnp.where(lo, 0.0, x2), axis=-1, keepdims=True)
    return jnp.where(lo, lax.rsqrt(s_lo / gs + EPS), lax.rsqrt(s_hi / gs + EPS))


def _epilogue(accs, vec_ref, epi, gs):
    if epi == "none":
        return accs[0]
    if epi == "silu_mul":
        g, u = accs
        return g * (1.0 / (1.0 + jnp.exp(-g))) * u
    y = accs[0]
    if epi == "logsig":
        z = y + vec_ref[2:3, :]
        ls = jnp.minimum(z, 0.0) - jnp.log(1.0 + jnp.exp(-jnp.abs(z)))
        return jnp.where(vec_ref[1:2, :] > 0, ls, 0.0)
    assert epi == "norm"
    cols = []
    for c in range(y.shape[1] // LANES):
        sl = slice(c * LANES, (c + 1) * LANES)
        x = y[:, sl]
        xn = x * _group_rms(x, gs) * vec_ref[0:1, sl]
        cols.append(jnp.where(vec_ref[1:2, sl] > 0, xn, x))
    return cols[0] if len(cols) == 1 else jnp.concatenate(cols, axis=1)


def _mm_kernel(*refs, n_w, cast_w, has_s, epi, has_res, n_out, gs, i_axis, cast_rows, k_perm):
    it = iter(refs)
    xp = next(it)
    xs = next(it) if has_s else None
    ws = [next(it) for _ in range(n_w)]
    vec = next(it) if epi in ("norm", "logsig") else None
    rp = next(it) if has_res else None
    rs = next(it) if (has_res and has_s) else None
    outs_p = [next(it) for _ in range(n_out)]
    outs_s = [next(it) for _ in range(n_out)] if has_s else []
    wbs = [next(it) for _ in range(n_w)] if cast_w else ws
    i = pl.program_id(i_axis)

    if cast_w:
        @pl.when(i == 0)
        def _():
            for w, wb in zip(ws, wbs):
                for cb in range(w.shape[0] // cast_rows):
                    src = k_perm[cb] if k_perm is not None else cb
                    wb[cb * cast_rows:(cb + 1) * cast_rows, :] = (
                        w[src * cast_rows:(src + 1) * cast_rows, :].astype(_CD))

    def emit(x_ref, r_ref, o_refs):
        y = _epilogue([_dot(x_ref[...], wb[...]) for wb in wbs], vec, epi, gs)
        if r_ref is not None:
            y = y + r_ref[...]
        for o in o_refs:
            o[...] = y.astype(o.dtype)

    emit(xp, rp, outs_p)
    if has_s:
        @pl.when(i == 0)
        def _():
            emit(xs, rs, outs_s)


def _cast_rows(k):
    for r in (512, 256, 128, 64, 32, 16):
        if k % r == 0:
            return r
    raise ValueError(k)


def _mm(xp, xs, ws, layer, n_cols, *, bm, bn, epi="none", vec=None, res=None,
        out_dtypes=(F32,), gs=LANES, k_perm=None, order="ws"):
    mp, k = xp.shape
    has_s = xs is not None
    cast_w = order == "ws"
    assert cast_w or (ws[0].dtype == _CD and not has_s)
    bm = min(bm, mp)
    bn = min(bn, n_cols)
    assert mp % bm == 0 and n_cols % bn == 0
    ni, nj = mp // bm, n_cols // bn
    if order == "ws":
        grid, i_axis = (nj, ni), 1
        ij = lambda a, b: (b, a)
    else:
        grid, i_axis = (ni, nj), 0
        ij = lambda a, b: (a, b)
    cast_rows = 512 if k_perm is not None else _cast_rows(k)

    def xmap(a, b):
        return (ij(a, b)[0], 0)

    def wmap(a, b):
        return (layer, 0, ij(a, b)[1])

    def omap(a, b):
        return ij(a, b)

    def smap(a, b):
        return (0, ij(a, b)[1])

    in_specs = [pl.BlockSpec((bm, k), xmap)]
    args = [xp]
    if has_s:
        ms = xs.shape[0]
        in_specs.append(pl.BlockSpec((ms, k), lambda a, b: (0, 0)))
        args.append(xs)
    for w in ws:
        in_specs.append(pl.BlockSpec((None, k, bn), wmap))
        args.append(w)
    if vec is not None:
        in_specs.append(pl.BlockSpec((8, bn), smap))
        args.append(vec)
    has_res = res is not None
    if has_res:
        in_specs.append(pl.BlockSpec((bm, bn), omap))
        args.append(res[0])
        if has_s:
            in_specs.append(pl.BlockSpec((ms, bn), smap))
            args.append(res[1])
    out_specs, out_shape = [], []
    for dt in out_dtypes:
        out_specs.append(pl.BlockSpec((bm, bn), omap))
        out_shape.append(jax.ShapeDtypeStruct((mp, n_cols), dt))
    if has_s:
        for dt in out_dtypes:
            out_specs.append(pl.BlockSpec((ms, bn), smap))
            out_shape.append(jax.ShapeDtypeStruct((ms, n_cols), dt))
    scratch = [pltpu.VMEM((k, bn), _CD) for _ in ws] if cast_w else []
    kern = functools.partial(
        _mm_kernel, n_w=len(ws), cast_w=cast_w, has_s=has_s, epi=epi, has_res=has_res,
        n_out=len(out_dtypes), gs=gs, i_axis=i_axis, cast_rows=cast_rows, k_perm=k_perm)
    return pl.pallas_call(
        kern, grid=grid, in_specs=in_specs, out_specs=out_specs, out_shape=out_shape,
        scratch_shapes=scratch, compiler_params=_cparams(2), name="proj_" + epi,
    )(*args)


def _vec(n, gain=None, flag=None, bias=None):
    v = jnp.zeros((8, n), F32)
    if gain is not None:
        v = v.at[0].set(gain.astype(F32))
    if flag is not None:
        v = v.at[1].set(flag.astype(F32))
    if bias is not None:
        v = v.at[2].set(bias.astype(F32))
    return v


def _normed_proj_kernel(x_ref, g_ref, w_ref, vec_ref, o_ref, wb_ref, *, gs):
    @pl.when(pl.program_id(0) == 0)
    def _():
        wb_ref[...] = w_ref[...].astype(_CD)

    x = x_ref[...]
    ms = jnp.mean(x * x, axis=-1, keepdims=True)
    h = (x * lax.rsqrt(ms + EPS) * g_ref[...]).astype(_CD)
    y = _epilogue([_dot(h, wb_ref[...])], vec_ref, "norm", gs)
    o_ref[...] = y.astype(o_ref.dtype)


def _normed_proj(x, g, w, layer, vec, *, bm):
    m, d = x.shape
    n = w.shape[2]
    bm = min(bm, m)
    return pl.pallas_call(
        functools.partial(_normed_proj_kernel, gs=LANES),
        grid=(m // bm,),
        in_specs=[pl.BlockSpec((bm, d), lambda i: (i, 0)),
                  pl.BlockSpec((1, d), lambda i: (0, 0)),
                  pl.BlockSpec((None, d, n), lambda i: (layer, 0, 0)),
                  pl.BlockSpec((8, n), lambda i: (0, 0))],
        out_specs=pl.BlockSpec((bm, n), lambda i: (i, 0)),
        out_shape=jax.ShapeDtypeStruct((m, n), _CD),
        scratch_shapes=[pltpu.VMEM((d, n), _CD)],
        compiler_params=_cparams(1),
        name="normed_proj",
    )(x, g.reshape(1, d).astype(F32), w, vec)


def _proj_res_norm_kernel(a_ref, w_ref, r_ref, g_ref, x_ref, h_ref, wb_ref):
    @pl.when(pl.program_id(0) == 0)
    def _():
        wb_ref[...] = w_ref[...].astype(_CD)

    x = r_ref[...] + _dot(a_ref[...], wb_ref[...])
    x_ref[...] = x
    ms = jnp.mean(x * x, axis=-1, keepdims=True)
    h_ref[...] = (x * lax.rsqrt(ms + EPS) * g_ref[...]).astype(h_ref.dtype)


def _proj_res_norm(a, w, layer, res, g, *, bm):
    m, k = a.shape
    d = w.shape[2]
    bm = min(bm, m)
    return pl.pallas_call(
        _proj_res_norm_kernel,
        grid=(m // bm,),
        in_specs=[pl.BlockSpec((bm, k), lambda i: (i, 0)),
                  pl.BlockSpec((None, k, d), lambda i: (layer, 0, 0)),
                  pl.BlockSpec((bm, d), lambda i: (i, 0)),
                  pl.BlockSpec((1, d), lambda i: (0, 0))],
        out_specs=[pl.BlockSpec((bm, d), lambda i: (i, 0)), pl.BlockSpec((bm, d), lambda i: (i, 0))],
        out_shape=[jax.ShapeDtypeStruct((m, d), F32), jax.ShapeDtypeStruct((m, d), _CD)],
        scratch_shapes=[pltpu.VMEM((k, d), _CD)],
        compiler_params=_cparams(1),
        name="proj_res_norm",
    )(a, w, res, g.reshape(1, d).astype(F32))


def _rel_bucket(n):
    max_exact = NUM_BUCKETS // 2
    nf = jnp.maximum(n, 1).astype(F32)
    large = max_exact + (jnp.log(nf / max_exact) / math.log(REL_MAX_DIST / max_exact)
                         * (NUM_BUCKETS - max_exact)).astype(jnp.int32)
    large = jnp.minimum(large, NUM_BUCKETS - 1)
    return jnp.where(n < max_exact, n, large)


def _rel_bias(table, n):
    bucket = _rel_bucket(jnp.maximum(n, 0))
    onehot = (bucket[..., None] == jnp.arange(NUM_BUCKETS)).astype(F32)
    out = jnp.tensordot(onehot, table.astype(F32), axes=1, precision=lax.Precision.HIGHEST)
    return jnp.moveaxis(out, -1, 0)


def _cumsum_kernel(x_ref, o_ref, *, blk):
    t = x_ref.shape[0]
    r = lax.broadcasted_iota(jnp.int32, (blk, blk), 0)
    c = lax.broadcasted_iota(jnp.int32, (blk, blk), 1)
    tri = (c <= r).astype(F32)

    def body(b, carry):
        rows = pl.ds(pl.multiple_of(b * blk, blk), blk)
        y = jnp.dot(tri, x_ref[rows, :], preferred_element_type=F32,
                    precision=lax.Precision.HIGHEST) + carry
        o_ref[rows, :] = y
        return y[blk - 1:blk, :]

    lax.fori_loop(0, t // blk, body, jnp.zeros((1, LANES), F32))


def _cumsum_time(x):
    b, t, w = x.shape
    blk = math.gcd(t, 256)
    return pl.pallas_call(
        functools.partial(_cumsum_kernel, blk=blk),
        grid=(b,),
        in_specs=[pl.BlockSpec((None, t, w), lambda n: (n, 0, 0))],
        out_specs=pl.BlockSpec((None, t, w), lambda n: (n, 0, 0)),
        out_shape=jax.ShapeDtypeStruct((b, t, w), F32),
        compiler_params=_cparams(1),
        name="logf_cumsum",
    )(x)


def _diff_lambda(lam_ref, lam_init):
    la = lam_ref[...]
    e1 = jnp.exp(jnp.sum(la[0:1, :] * la[1:2, :], axis=-1, keepdims=True))
    e2 = jnp.exp(jnp.sum(la[2:3, :] * la[3:4, :], axis=-1, keepdims=True))
    return e1 - e2 + lam_init


def _subln(o, sub_ref, lam_init):
    ms = jnp.mean(o * o, axis=-1, keepdims=True)
    return o * lax.rsqrt(ms + EPS) * sub_ref[...] * (1.0 - lam_init)


def _even_prompt_kernel(aq_ref, ak_ref, av_ref, bq_ref, bk_ref, bv_ref, cq_ref, ck_ref,
                        abias_ref, fmask_ref, lam_ref, sub_ref, o_ref,
                        m_ref, l_ref, acca_ref, accb_ref, *, lam_init):
    tq, tk = TQ_A, TK_A
    qi = pl.program_id(2)
    jd = (qi * tq) // tk
    m_ref[...] = jnp.full(m_ref.shape, M_INIT, F32)
    l_ref[...] = jnp.zeros(l_ref.shape, F32)
    acca_ref[...] = jnp.zeros(acca_ref.shape, F32)
    accb_ref[...] = jnp.zeros(accb_ref.shape, F32)

    def update(s, v, acc_ref, sidx, aidx):
        m_prev = m_ref[sidx]
        m_new = jnp.maximum(m_prev, jnp.max(s, axis=1, keepdims=True))
        p = jnp.exp(s - _rep(m_new, s.shape[1]))
        alpha = jnp.exp(m_prev - m_new)
        l_ref[sidx] = alpha * l_ref[sidx] + jnp.sum(p, axis=1, keepdims=True)
        m_ref[sidx] = m_new
        acc_ref[aidx] = acc_ref[aidx] * _rep(alpha, v.shape[1]) + _dot(p.astype(_CD), v)

    def step(j, diag):
        ks = pl.ds(pl.multiple_of(j * tk, tk), tk)
        ka = ak_ref[ks, :]
        va = av_ref[ks, :]
        off = jnp.minimum((qi * tq - j * tk) // LANES, 3)
        for g in range(A_G):
            bias = abias_ref[g, off]
            for c in range(2):
                q = aq_ref[:, (g * 2 + c) * HD:(g * 2 + c + 1) * HD]
                s = _dot_nt(q, ka[:, c * HD:(c + 1) * HD]) * SCALE + bias
                update(s, va, acca_ref, g * 2 + c, g * 2 + c)
        kb = bk_ref[ks, :]
        vb = bv_ref[ks, :]
        for g in range(B_G):
            q = bq_ref[:, g * HD:(g + 1) * HD]
            s = _dot_nt(q, kb) * SCALE + (cq_ref[:, g:g + 1] - ck_ref[g, pl.ds(j, 1), :])
            if diag:
                s = s + fmask_ref[off]
            update(s, vb, accb_ref, 4 + g, g)

    def body(j, carry):
        step(j, False)
        return carry

    lax.fori_loop(0, jd, body, 0)
    step(jd, True)

    lam = _diff_lambda(lam_ref, lam_init)
    for g in range(A_G):
        o1 = acca_ref[2 * g] / _rep(l_ref[2 * g], 2 * HD)
        o2 = acca_ref[2 * g + 1] / _rep(l_ref[2 * g + 1], 2 * HD)
        o = _subln(o1 - lam * o2, sub_ref, lam_init)
        o_ref[:, g * 2 * HD:(g + 1) * 2 * HD] = o.astype(o_ref.dtype)
    for g in range(B_G):
        o = accb_ref[g] / l_ref[4 + g]
        o_ref[:, A_G * 2 * HD + g * HD:A_G * 2 * HD + (g + 1) * HD] = o.astype(o_ref.dtype)


def _even_prompt_attn(pb, cq, ck, abias, fmask, lam_p, subln, lam_init):
    b, t, _ = pb.shape
    tq, tk = TQ_A, TK_A
    nq, nkb = t // tq, t // tk
    o_w = A_G * 2 * HD + B_G * HD
    ak0 = A_Q_W // (2 * HD)
    av0 = (A_Q_W + A_KV_W) // (2 * HD)
    bq0 = (A_Q_W + 2 * A_KV_W) // (B_G * HD)
    bk0 = (A_Q_W + 2 * A_KV_W + B_Q_W) // HD
    bv0 = bk0 + B_KVH
    in_specs = [
        pl.BlockSpec((None, tq, A_G * 2 * HD), lambda n, h, q: (n, q, h)),
        pl.BlockSpec((None, t, 2 * HD), lambda n, h, q: (n, 0, ak0 + h)),
        pl.BlockSpec((None, t, 2 * HD), lambda n, h, q: (n, 0, av0 + h)),
        pl.BlockSpec((None, tq, B_G * HD), lambda n, h, q: (n, q, bq0 + h)),
        pl.BlockSpec((None, t, HD), lambda n, h, q: (n, 0, bk0 + h)),
        pl.BlockSpec((None, t, HD), lambda n, h, q: (n, 0, bv0 + h)),
        pl.BlockSpec((None, None, tq, B_G), lambda n, h, q: (n, h, q, 0)),
        pl.BlockSpec((None, None, B_G, nkb, tk), lambda n, h, q: (n, h, 0, 0, 0)),
        pl.BlockSpec((A_G, 4, tq, tk), lambda n, h, q: (h, 0, 0, 0)),
        pl.BlockSpec((2, tq, tk), lambda n, h, q: (0, 0, 0)),
        pl.BlockSpec((4, HD), lambda n, h, q: (0, 0)),
        pl.BlockSpec((1, 2 * HD), lambda n, h, q: (0, 0)),
    ]
    return pl.pallas_call(
        functools.partial(_even_prompt_kernel, lam_init=lam_init),
        grid=(b, A_KVH, nq),
        in_specs=in_specs,
        out_specs=pl.BlockSpec((None, tq, o_w), lambda n, h, q: (n, q, h)),
        out_shape=jax.ShapeDtypeStruct((b, t, A_KVH * o_w), _CD),
        scratch_shapes=[pltpu.VMEM((8, tq, LANES), F32), pltpu.VMEM((8, tq, LANES), F32),
                        pltpu.VMEM((4, tq, 2 * HD), F32), pltpu.VMEM((4, tq, HD), F32)],
        compiler_params=_cparams(3),
        name="even_prompt_attn",
    )(pb, pb, pb, pb, pb, pb, cq, ck, abias, fmask, lam_p, subln)


def _even_decode_kernel(pt_ref, qa_ref, qb_ref, kan_ref, van_ref, kbn_ref, vbn_ref, lfn_ref,
                        abias_ref, bmask_ref, later_ref, aself_ref, lam_ref, sub_ref, *rest,
                        n_chunks, lam_init):
    del pt_ref
    npg = PAGES_PER_STEP
    pages = rest[:5 * npg]
    o_ref = rest[5 * npg]
    m_ref, l_ref, acca_ref, accb_ref, run_ref = rest[5 * npg + 1:]
    c = pl.program_id(1)

    @pl.when(c == 0)
    def _():
        sa = jnp.sum(qa_ref[...].astype(F32) * kan_ref[...].astype(F32), axis=1, keepdims=True)
        sb = jnp.sum(qb_ref[...].astype(F32) * kbn_ref[...].astype(F32), axis=1, keepdims=True)
        m_ref[0] = sa * SCALE + aself_ref[...]
        m_ref[1] = jnp.broadcast_to(sb * SCALE, (16, LANES))
        l_ref[...] = jnp.ones(l_ref.shape, F32)
        acca_ref[...] = van_ref[...].astype(F32)
        accb_ref[...] = vbn_ref[...].astype(F32)
        run_ref[...] = lfn_ref[...]

    page = pages[0].shape[0]
    cols = page * A_KVH
    tdims = (((0,), (0,)), ((), ()))

    def update(idx, s, v, acc_ref):
        m_prev = m_ref[idx]
        m_new = jnp.maximum(m_prev, jnp.max(s, axis=1, keepdims=True))
        pr = jnp.exp(s - _rep(m_new, s.shape[1]))
        alpha = jnp.exp(m_prev - m_new)
        l_ref[idx] = alpha * l_ref[idx] + jnp.sum(pr, axis=1, keepdims=True)
        m_ref[idx] = m_new
        acc_ref[...] = acc_ref[...] * _rep(alpha, v.shape[1]) + _dot(pr.astype(_CD), v)

    for p in range(npg):
        ka_ref, va_ref, kb_ref, vb_ref, lf_ref = pages[5 * p:5 * p + 5]
        lf = lf_ref[...]
        hi = lf.astype(_CD)
        r1 = lf - hi.astype(F32)
        mid = r1.astype(_CD)
        lo = (r1 - mid.astype(F32)).astype(_CD)
        sums = lax.dot_general(jnp.concatenate([hi, mid, lo], axis=1), later_ref[...], tdims,
                               preferred_element_type=F32)
        sums = sums[0:16] + sums[16:32] + sums[32:48]
        run = run_ref[...]
        fbias = _rep(run, cols) + sums[:, :cols] + bmask_ref[...]
        run_ref[...] = run + sums[:, cols:]
        xa = ka_ref[...].reshape(cols, 2 * HD).astype(_CD)
        sa = _dot_nt(qa_ref[...], xa) * SCALE + abias_ref[p]
        update(0, sa, va_ref[...].reshape(cols, 2 * HD).astype(_CD), acca_ref)
        xb = kb_ref[...].reshape(cols, HD).astype(_CD)
        sb = _dot_nt(qb_ref[...], xb) * SCALE + fbias
        update(1, sb, vb_ref[...].reshape(cols, HD).astype(_CD), accb_ref)

    @pl.when(c == n_chunks - 1)
    def _():
        lam = _diff_lambda(lam_ref, lam_init)
        o_w = A_G * 2 * HD + B_G * HD
        la = l_ref[0]
        lb = l_ref[1]
        for h in range(A_KVH):
            for g in range(A_G):
                r1 = h * 4 + g * 2
                o1 = acca_ref[r1:r1 + 1, :] / _rep(la[r1:r1 + 1, :], 2 * HD)
                o2 = acca_ref[r1 + 1:r1 + 2, :] / _rep(la[r1 + 1:r1 + 2, :], 2 * HD)
                o = _subln(o1 - lam * o2, sub_ref, lam_init)
                o_ref[0:1, h * o_w + g * 2 * HD:h * o_w + (g + 1) * 2 * HD] = o
            for g in range(B_G):
                rr = h * 4 + g
                o = accb_ref[rr:rr + 1, :] / lb[rr:rr + 1, :]
                o_ref[0:1, h * o_w + A_G * 2 * HD + g * HD:h * o_w + A_G * 2 * HD + (g + 1) * HD] = o


def _even_decode_attn(e, page_table, caches, qa16, qb16, new_kv, lf_new, abias, bmask, later,
                      aself, lam_p, subln, lam_init):
    cache_a_k, cache_a_v, cache_b_k, cache_b_v, cache_b_logf = caches
    db, n_pages = page_table.shape
    page = cache_a_k.shape[2]
    npg = PAGES_PER_STEP
    assert n_pages % npg == 0
    n_chunks = n_pages // npg
    kan, van, kbn, vbn = new_kv
    cols = page * A_KVH

    def fixed(shape):
        nd = len(shape)
        return pl.BlockSpec((None,) + shape, lambda b, c, pt: (b,) + (0,) * nd)

    def const(shape):
        nd = len(shape)
        return pl.BlockSpec(shape, lambda b, c, pt: (0,) * nd)

    in_specs = [
        fixed((16, 2 * HD)), fixed((16, HD)),
        fixed((16, 2 * HD)), fixed((16, 2 * HD)), fixed((16, HD)), fixed((16, HD)),
        fixed((16, LANES)),
        pl.BlockSpec((npg, 16, cols), lambda b, c, pt: (c, 0, 0)),
        const((16, cols)), const((page, cols + LANES)),
        const((16, LANES)), const((4, HD)), const((1, 2 * HD)),
    ]
    args = [qa16, qb16, kan, van, kbn, vbn, lf_new, abias, bmask, later, aself, lam_p, subln]
    for p in range(npg):
        def pmap(b, c, pt, p=p):
            return (e, pt[b, n_pages - 1 - (c * npg + p)], 0, 0, 0)

        def fmap(b, c, pt, p=p):
            return (e, pt[b, n_pages - 1 - (c * npg + p)], 0, 0)

        in_specs += [
            pl.BlockSpec((None, None, page, A_KVH, 2 * HD), pmap),
            pl.BlockSpec((None, None, page, A_KVH, 2 * HD), pmap),
            pl.BlockSpec((None, None, page, B_KVH, HD), pmap),
            pl.BlockSpec((None, None, page, B_KVH, HD), pmap),
            pl.BlockSpec((None, None, page, B_HEADS), fmap),
        ]
        args += [cache_a_k, cache_a_v, cache_b_k, cache_b_v, cache_b_logf]
    o_w = A_KVH * (A_G * 2 * HD + B_G * HD)
    out = pl.pallas_call(
        functools.partial(_even_decode_kernel, n_chunks=n_chunks, lam_init=lam_init),
        grid_spec=pltpu.PrefetchScalarGridSpec(
            num_scalar_prefetch=1,
            grid=(db, n_chunks),
            in_specs=in_specs,
            out_specs=pl.BlockSpec((None, 1, o_w), lambda b, c, pt: (b, 0, 0)),
            scratch_shapes=[pltpu.VMEM((2, 16, LANES), F32), pltpu.VMEM((2, 16, LANES), F32),
                            pltpu.VMEM((16, 2 * HD), F32), pltpu.VMEM((16, HD), F32),
                            pltpu.VMEM((16, LANES), F32)],
        ),
        out_shape=jax.ShapeDtypeStruct((db, 1, o_w), F32),
        compiler_params=_cparams(2),
        name="even_decode_attn",
    )(page_table, *args)
    return out.reshape(db, o_w)


def _swa_prompt_kernel(sink_ref, q_ref, kp_ref, kc_ref, vp_ref, vc_ref, bias_ref, o_ref):
    qi = pl.program_id(1)
    kp = pl.program_id(2)
    w = WINDOW
    k_cat = jnp.concatenate([kp_ref[...], kc_ref[...]], axis=0)
    v_cat = jnp.concatenate([vp_ref[...], vc_ref[...]], axis=0)
    col = lax.broadcasted_iota(jnp.int32, (w, 2 * w), 1)
    keep = (col >= w) | (qi > 0)
    outs = []
    for kv in range(2):
        k_h = k_cat[:, kv * C_HD:(kv + 1) * C_HD]
        v_h = v_cat[:, kv * C_HD:(kv + 1) * C_HD]
        for g in range(C_G):
            hh = kv * C_G + g
            q = q_ref[:, hh * C_HD:(hh + 1) * C_HD]
            s = _dot_nt(q, k_h) + jnp.where(keep, bias_ref[hh], NEG)
            sink = sink_ref[kp * 2 * C_G + hh]
            m = jnp.maximum(jnp.max(jnp.maximum(s[:, :w], s[:, w:]), axis=1, keepdims=True), sink)
            e = jnp.exp(s - m)
            den = jnp.sum(e[:, :w] + e[:, w:], axis=1, keepdims=True) + jnp.exp(sink - m)
            outs.append(_dot((e * (1.0 / den)).astype(_CD), v_h))
    o_ref[...] = jnp.concatenate(outs, axis=1).astype(o_ref.dtype)


def _swa_prompt_attn(pb, bias, sinks):
    b, t, _ = pb.shape
    w = WINDOW
    nq = t // w
    qw = 2 * C_G * C_HD
    k0 = C_Q_W // LANES
    v0 = (C_Q_W + C_KV_W) // LANES
    prev = lambda q: jnp.maximum(q - 1, 0)
    in_specs = [
        pl.BlockSpec(memory_space=pltpu.SMEM),
        pl.BlockSpec((None, w, qw), lambda n, q, k: (n, q, k)),
        pl.BlockSpec((None, w, LANES), lambda n, q, k: (n, prev(q), k0 + k)),
        pl.BlockSpec((None, w, LANES), lambda n, q, k: (n, q, k0 + k)),
        pl.BlockSpec((None, w, LANES), lambda n, q, k: (n, prev(q), v0 + k)),
        pl.BlockSpec((None, w, LANES), lambda n, q, k: (n, q, v0 + k)),
        pl.BlockSpec((2 * C_G, w, 2 * w), lambda n, q, k: (k, 0, 0)),
    ]
    return pl.pallas_call(
        _swa_prompt_kernel,
        grid=(b, nq, C_KVH // 2),
        in_specs=in_specs,
        out_specs=pl.BlockSpec((None, w, qw), lambda n, q, k: (n, q, k)),
        out_shape=jax.ShapeDtypeStruct((b, t, C_Q_W), _CD),
        compiler_params=_cparams(3),
        name="swa_prompt_attn",
    )(sinks.astype(F32), pb, pb, pb, pb, pb, bias)


def _swa_decode_kernel(q_ref, kbuf_ref, vbuf_ref, kn_ref, vn_ref, bias_ref, bself_ref, sink_ref,
                       o_ref):
    cols = kbuf_ref.shape[0] * C_KVH
    q = q_ref[...]
    s = _dot_nt(q, kbuf_ref[...].reshape(cols, C_HD).astype(_CD)) + bias_ref[...]
    s_new = (jnp.sum(q.astype(F32) * kn_ref[...].astype(F32), axis=1, keepdims=True)
             + bself_ref[...])
    sink = sink_ref[...]
    m = jnp.maximum(jnp.maximum(jnp.max(s, axis=1, keepdims=True), s_new), sink)
    e = jnp.exp(s - m)
    e_new = jnp.exp(s_new - m)
    den = jnp.sum(e, axis=1, keepdims=True) + e_new + jnp.exp(sink - m)
    inv = 1.0 / den
    o = (e_new * inv).astype(_CD).astype(F32) * vn_ref[...].astype(F32)
    o_ref[...] = o + _dot((e * inv).astype(_CD), vbuf_ref[...].reshape(cols, C_HD).astype(_CD))


def _swa_decode_attn(o_idx, q64, cache_c_k, cache_c_v, kn, vn, bias, bself, sinks):
    db = q64.shape[0]
    cbuf = cache_c_k.shape[2]
    in_specs = [
        pl.BlockSpec((None, C_HEADS, C_HD), lambda b: (b, 0, 0)),
        pl.BlockSpec((None, None, cbuf, C_KVH, C_HD), lambda b: (o_idx, b, 0, 0, 0)),
        pl.BlockSpec((None, None, cbuf, C_KVH, C_HD), lambda b: (o_idx, b, 0, 0, 0)),
        pl.BlockSpec((None, C_HEADS, C_HD), lambda b: (b, 0, 0)),
        pl.BlockSpec((None, C_HEADS, C_HD), lambda b: (b, 0, 0)),
        pl.BlockSpec((C_HEADS, cbuf * C_KVH), lambda b: (0, 0)),
        pl.BlockSpec((C_HEADS, 1), lambda b: (0, 0)),
        pl.BlockSpec((C_HEADS, 1), lambda b: (0, 0)),
    ]
    out = pl.pallas_call(
        _swa_decode_kernel,
        grid=(db,),
        in_specs=in_specs,
        out_specs=pl.BlockSpec((None, C_HEADS, C_HD), lambda b: (b, 0, 0)),
        out_shape=jax.ShapeDtypeStruct((db, C_HEADS, C_HD), F32),
        compiler_params=_cparams(1),
        name="swa_decode_attn",
    )(q64, cache_c_k, cache_c_v, kn, vn, bias, bself, sinks)
    return out.reshape(db, C_Q_W)


def _cross_prompt_kernel(q_ref, k_ref, v_ref, o_ref):
    for h in range(M_HEADS):
        sl = slice(h * HD, (h + 1) * HD)
        s = _dot_nt(q_ref[:, sl], k_ref[:, sl]) * SCALE
        m = jnp.max(s, axis=1, keepdims=True)
        e = jnp.exp(s - m)
        p = e * (1.0 / jnp.sum(e, axis=1, keepdims=True))
        o_ref[:, sl] = _dot(p.astype(_CD), v_ref[:, sl]).astype(o_ref.dtype)


def _cross_prompt_attn(q, mkv):
    b, t, _ = q.shape
    nm = mkv.shape[1]
    tq = math.gcd(t, 512)
    return pl.pallas_call(
        _cross_prompt_kernel,
        grid=(b, t // tq),
        in_specs=[pl.BlockSpec((None, tq, M_W), lambda n, i: (n, i, 0)),
                  pl.BlockSpec((None, nm, M_W), lambda n, i: (n, 0, 0)),
                  pl.BlockSpec((None, nm, M_W), lambda n, i: (n, 0, 1))],
        out_specs=pl.BlockSpec((None, tq, M_W), lambda n, i: (n, i, 0)),
        out_shape=jax.ShapeDtypeStruct((b, t, M_W), _CD),
        compiler_params=_cparams(2),
        name="cross_prompt_attn",
    )(q, mkv, mkv)


def _cross_decode_kernel(q_ref, k_ref, v_ref, mask_ref, o_ref):
    cols = k_ref.shape[0] * M_HEADS
    s = _dot_nt(q_ref[...], k_ref[...].reshape(cols, HD).astype(_CD)) * SCALE + mask_ref[...]
    m = jnp.max(s, axis=1, keepdims=True)
    e = jnp.exp(s - m)
    p = (e * (1.0 / jnp.sum(e, axis=1, keepdims=True))).astype(_CD)
    o_ref[...] = _dot(p, v_ref[...].reshape(cols, HD).astype(_CD))


def _cross_decode_attn(layer, q8, cache_m_k, cache_m_v):
    db = q8.shape[0]
    nm = cache_m_k.shape[2]
    cols = nm * M_HEADS
    mask = jnp.where((jnp.arange(8) % M_HEADS)[:, None] == (jnp.arange(cols) % M_HEADS)[None, :],
                     0.0, NEG).astype(F32)
    out = pl.pallas_call(
        _cross_decode_kernel,
        grid=(db,),
        in_specs=[pl.BlockSpec((None, 8, HD), lambda b: (b, 0, 0)),
                  pl.BlockSpec((None, None, nm, M_HEADS, HD), lambda b: (layer, b, 0, 0, 0)),
                  pl.BlockSpec((None, None, nm, M_HEADS, HD), lambda b: (layer, b, 0, 0, 0)),
                  pl.BlockSpec((8, cols), lambda b: (0, 0))],
        out_specs=pl.BlockSpec((None, 8, HD), lambda b: (b, 0, 0)),
        out_shape=jax.ShapeDtypeStruct((db, 8, HD), F32),
        compiler_params=_cparams(1),
        name="cross_decode_attn",
    )(q8, cache_m_k, cache_m_v, mask)
    return out[:, :M_HEADS, :].reshape(db, M_W)


def kernel(x_prompt, x_sample, mem_prompt, cache_a_k, cache_a_v, cache_b_k, cache_b_v, cache_b_logf, cache_c_k, cache_c_v, cache_m_k, cache_m_v, page_table, rel_table, norm_mix, norm_cross, norm_mem, norm_ffn, w_in_even, b_forget, a_q_norm, a_k_norm, a_lambda, a_subln, b_q_norm, b_k_norm, w_out_even, w_in_odd, c_q_norm, c_k_norm, c_sinks, w_out_odd, w_mq, w_mkv, m_q_norm, m_k_norm, w_mo, w_gate, w_up, w_down):
    b, t, d = x_prompt.shape
    db = x_sample.shape[0]
    assert x_sample.shape[1] == 1
    depth = norm_mix.shape[0]
    n_mem = mem_prompt.shape[1]
    n_pages = page_table.shape[1]
    page = cache_a_k.shape[2]
    past_len = n_pages * page
    cbuf = cache_c_k.shape[2]
    d_ff = w_gate.shape[2]
    mp = b * t
    assert t % TK_A == 0 and t % WINDOW == 0 and cbuf == WINDOW and t >= cbuf

    table_a = rel_table[:, :A_HEADS]
    table_c = rel_table[:, A_HEADS:]

    rr = jnp.arange(TQ_A)[:, None]
    cc = jnp.arange(TK_A)[None, :]
    tiles, masks = [], []
    for u in range(4):
        n = u * LANES + rr - cc
        bt = _rel_bias(table_a, n)
        if u * LANES < TK_A - 1:
            bt = jnp.where((n >= 0)[None], bt, NEG)
            masks.append(jnp.where(n >= 0, 0.0, NEG).astype(F32))
        tiles.append(bt)
    assert 3 * LANES - (TK_A - 1) >= REL_MAX_DIST
    abias_p = jnp.stack(tiles, axis=1)
    fmask_p = jnp.stack(masks, axis=0)

    dist_past = past_len - jnp.arange(past_len)
    row_head = jnp.array([h * A_G + g for h in range(A_KVH) for g in range(A_G) for _ in range(2)])
    ab = _rel_bias(table_a, dist_past)[row_head]
    ab = ab.reshape(16, n_pages, page).transpose(1, 0, 2)[::-1]
    cols = page * A_KVH
    head_mask = jnp.where((jnp.arange(16) // 4)[:, None] == (jnp.arange(cols) % A_KVH)[None, :],
                          0.0, NEG).astype(F32)
    ab = jnp.repeat(ab, A_KVH, axis=2) + head_mask[None]
    later = jnp.concatenate(
        [(jnp.arange(page)[:, None] > (jnp.arange(cols) // A_KVH)[None, :]).astype(_CD),
         jnp.ones((page, LANES), _CD)], axis=1)
    aself = jnp.broadcast_to(_rel_bias(table_a, jnp.zeros((1,), jnp.int32))[row_head], (16, LANES))

    wi = jnp.arange(WINDOW)[:, None]
    wj = jnp.arange(2 * WINDOW)[None, :]
    nband = wi + WINDOW - wj
    cbias_p = jnp.where(((nband >= 0) & (nband <= WINDOW))[None], _rel_bias(table_c, nband), NEG)
    dist_buf = cbuf - jnp.arange(cbuf)
    cbias_d = jnp.where((dist_buf <= WINDOW)[None], _rel_bias(table_c, dist_buf), NEG)
    ccols = cbuf * C_KVH
    cbias_d = jnp.repeat(cbias_d, C_KVH, axis=1) + jnp.where(
        (jnp.arange(C_HEADS) // C_G)[:, None] == (jnp.arange(ccols) % C_KVH)[None, :], 0.0, NEG)
    cself = _rel_bias(table_c, jnp.zeros((1,), jnp.int32))

    xp = x_prompt.reshape(mp, d)
    xs = x_sample.reshape(db, d)
    outs = {k: [] for k in ("ak_p", "ak_s", "av_p", "av_s", "bk_p", "bk_s", "bv_p", "bv_s",
                            "lf_p", "lf_s", "ck_p", "ck_s", "cv_p", "cv_s", "mk", "mv")}
    w_down_c = w_down.astype(_CD)
    mem2 = mem_prompt.reshape(b * n_mem, d)

    for l in range(depth):
        hp = _rmsnorm(xp, norm_mix[l])
        hs = _rmsnorm(xs, norm_mix[l])
        if l % 2 == 0:
            e = l // 2
            lam_init = 0.8 - 0.6 * math.exp(-0.3 * l)
            gain = jnp.concatenate([
                jnp.tile(a_q_norm[e].reshape(-1), A_HEADS), jnp.tile(a_k_norm[e].reshape(-1), A_KVH),
                jnp.ones((A_KV_W,), F32), jnp.tile(b_q_norm[e], B_HEADS), jnp.tile(b_k_norm[e], B_KVH),
                jnp.ones((B_KV_W,), F32)])
            flag = jnp.concatenate([jnp.ones((A_Q_W + A_KV_W,)), jnp.zeros((A_KV_W,)),
                                    jnp.ones((B_Q_W + B_KV_W,)), jnp.zeros((B_KV_W,))])
            pf, pc, sf, sc = _mm(hp, hs, [w_in_even], e, EVEN_MAIN, bm=1024, bn=512, epi="norm",
                                 vec=_vec(EVEN_MAIN, gain, flag), out_dtypes=(F32, _CD))
            w_f = jnp.pad(w_in_even[e][:, EVEN_MAIN:], ((0, 0), (0, LANES - B_HEADS)))[None]
            fvec = _vec(LANES, flag=jnp.arange(LANES) < B_HEADS,
                        bias=jnp.pad(b_forget[e], (0, LANES - B_HEADS)))
            lf_p, lf_s = _mm(hp, hs, [w_f], 0, LANES, bm=1024, bn=LANES, epi="logsig", vec=fvec)

            o1, o2, o3, o4 = A_Q_W, A_Q_W + A_KV_W, A_Q_W + 2 * A_KV_W, A_Q_W + 2 * A_KV_W + B_Q_W
            o5 = o4 + B_KV_W
            outs["ak_p"].append(pf[:, o1:o2].reshape(b, t, A_KVH, 2 * HD))
            outs["av_p"].append(pf[:, o2:o3].reshape(b, t, A_KVH, 2 * HD))
            outs["bk_p"].append(pf[:, o4:o5].reshape(b, t, B_KVH, HD))
            outs["bv_p"].append(pf[:, o5:].reshape(b, t, B_KVH, HD))
            outs["lf_p"].append(lf_p[:, :B_HEADS].reshape(b, t, B_HEADS))
            outs["ak_s"].append(sf[:, o1:o2].reshape(db, 1, A_KVH, 2 * HD))
            outs["av_s"].append(sf[:, o2:o3].reshape(db, 1, A_KVH, 2 * HD))
            outs["bk_s"].append(sf[:, o4:o5].reshape(db, 1, B_KVH, HD))
            outs["bv_s"].append(sf[:, o5:].reshape(db, 1, B_KVH, HD))
            outs["lf_s"].append(lf_s[:, :B_HEADS].reshape(db, 1, B_HEADS))

            csum = _cumsum_time(lf_p.reshape(b, t, LANES))[:, :, :B_HEADS]
            cq = csum.reshape(b, t, B_KVH, B_G).transpose(0, 2, 1, 3)
            ck = csum.transpose(0, 2, 1).reshape(b, B_KVH, B_G, t // TK_A, TK_A)
            lam_p = a_lambda[e].astype(F32)
            subln = a_subln[e].reshape(1, 2 * HD).astype(F32)
            op = _even_prompt_attn(pc.reshape(b, t, EVEN_MAIN), cq, ck, abias_p, fmask_p,
                                   lam_p, subln, lam_init).reshape(mp, -1)

            qa16 = sc[:, :A_Q_W].reshape(db, A_KVH, A_G, 2, HD)
            eye2 = jnp.eye(2, dtype=_CD)
            qa16 = (qa16[:, :, :, :, None, :] * eye2[None, None, None, :, :, None]).reshape(
                db, 16, 2 * HD)
            qb16 = sc[:, o3:o4].reshape(db, 16, HD)
            per_row = lambda x, w: jnp.repeat(x.reshape(db, A_KVH, w), 4, axis=1)
            new_kv = (per_row(sc[:, o1:o2], 2 * HD), per_row(sc[:, o2:o3], 2 * HD),
                      per_row(sc[:, o4:o5], HD), per_row(sc[:, o5:], HD))
            lf_new = jnp.broadcast_to(lf_s[:, :B_HEADS, None], (db, B_HEADS, LANES))
            os_ = _even_decode_attn(e, page_table,
                                    (cache_a_k, cache_a_v, cache_b_k, cache_b_v, cache_b_logf),
                                    qa16, qb16, new_kv, lf_new, ab, head_mask, later, aself,
                                    lam_p, subln, lam_init)
            k_perm = [(kb % 2) * A_KVH + kb // 2 for kb in range(2 * A_KVH)]
            xp, xs = _mm(op, os_.astype(_CD), [w_out_even], e, d, bm=1024, bn=512,
                         res=(xp, xs), k_perm=k_perm)
        else:
            o_ = l // 2
            gain = jnp.concatenate([jnp.tile(c_q_norm[o_] * C_SCALE, C_HEADS),
                                    jnp.tile(c_k_norm[o_], C_KVH), jnp.ones((C_KV_W,), F32)])
            flag = jnp.concatenate([jnp.ones((C_Q_W + C_KV_W,)), jnp.zeros((C_KV_W,))])
            pf, pc, sf, sc = _mm(hp, hs, [w_in_odd], o_, ODD_IN, bm=1024, bn=512, epi="norm",
                                 vec=_vec(ODD_IN, gain, flag), out_dtypes=(F32, _CD), gs=C_HD)
            kf = pf[:, C_Q_W:C_Q_W + C_KV_W].reshape(b, t, C_KVH, C_HD)
            vf = pf[:, C_Q_W + C_KV_W:].reshape(b, t, C_KVH, C_HD)
            outs["ck_p"].append(kf[:, t - cbuf:])
            outs["cv_p"].append(vf[:, t - cbuf:])
            ks_new = sf[:, C_Q_W:C_Q_W + C_KV_W].reshape(db, 1, C_KVH, C_HD)
            vs_new = sf[:, C_Q_W + C_KV_W:].reshape(db, 1, C_KVH, C_HD)
            outs["ck_s"].append(jnp.concatenate([cache_c_k[o_][:, 1:], ks_new], axis=1))
            outs["cv_s"].append(jnp.concatenate([cache_c_v[o_][:, 1:], vs_new], axis=1))

            op = _swa_prompt_attn(pc.reshape(b, t, ODD_IN), cbias_p, c_sinks[o_]).reshape(mp, C_Q_W)
            q64 = sc[:, :C_Q_W].reshape(db, C_HEADS, C_HD)
            kn = jnp.repeat(sc[:, C_Q_W:C_Q_W + C_KV_W].reshape(db, C_KVH, C_HD), C_G, axis=1)
            vn = jnp.repeat(sc[:, C_Q_W + C_KV_W:].reshape(db, C_KVH, C_HD), C_G, axis=1)
            os_ = _swa_decode_attn(o_, q64, cache_c_k, cache_c_v, kn, vn, cbias_d, cself,
                                   c_sinks[o_].reshape(C_HEADS, 1).astype(F32))
            xp, xs = _mm(op, os_.astype(_CD), [w_out_odd], o_, d, bm=1024, bn=512, res=(xp, xs))

        hm = _rmsnorm(mem2, norm_mem[l])
        mgain = jnp.concatenate([jnp.tile(m_k_norm[l], M_HEADS), jnp.ones((M_W,), F32)])
        mflag = jnp.concatenate([jnp.ones((M_W,)), jnp.zeros((M_W,))])
        mf, mc = _mm(hm, None, [w_mkv], l, 2 * M_W, bm=1024, bn=512, epi="norm",
                     vec=_vec(2 * M_W, mgain, mflag), out_dtypes=(F32, _CD))
        outs["mk"].append(mf[:, :M_W].reshape(b, n_mem, M_HEADS, HD))
        outs["mv"].append(mf[:, M_W:].reshape(b, n_mem, M_HEADS, HD))
        qvec = _vec(M_W, jnp.tile(m_q_norm[l], M_HEADS), jnp.ones((M_W,)))
        qp = _normed_proj(xp, norm_cross[l], w_mq, l, qvec, bm=512)
        qs = _normed_proj(xs, norm_cross[l], w_mq, l, qvec, bm=512)
        op = _cross_prompt_attn(qp.reshape(b, t, M_W), mc.reshape(b, n_mem, 2 * M_W)).reshape(mp, M_W)
        q8 = jnp.pad(qs.reshape(db, M_HEADS, HD), ((0, 0), (0, 8 - M_HEADS), (0, 0)))
        os_ = _cross_decode_attn(l, q8, cache_m_k, cache_m_v)

        xp, hp = _proj_res_norm(op, w_mo, l, xp, norm_ffn[l], bm=256)
        xs, hs = _proj_res_norm(os_.astype(_CD), w_mo, l, xs, norm_ffn[l], bm=256)

        ap, as_ = _mm(hp, hs, [w_gate, w_up], l, d_ff, bm=1024, bn=256, epi="silu_mul",
                      out_dtypes=(_CD,))
        (xp,) = _mm(ap, None, [w_down_c], l, d, bm=512, bn=512, res=(xp,), order="xs")
        (xs,) = _mm(as_, None, [w_down_c], l, d, bm=512, bn=512, res=(xs,), order="xs")

    st = lambda k: jnp.stack(outs[k])
    return (xp.reshape(b, t, d), xs.reshape(db, 1, d),
            st("ak_p"), st("ak_s"), st("av_p"), st("av_s"), st("bk_p"), st("bk_s"),
            st("bv_p"), st("bv_s"), st("lf_p"), st("lf_s"), st("ck_p"), st("ck_s"),
            st("cv_p"), st("cv_s"), st("mk"), st("mv"))
```

```python
import functools
import math

import jax
import jax.numpy as jnp
from jax import lax
from jax.experimental import pallas as pl
from jax.experimental.pallas import tpu as pltpu

F32 = jnp.float32
_CD = jnp.bfloat16

HD = 128
A_HEADS, A_KVH = 8, 4
A_G = A_HEADS // A_KVH
B_HEADS, B_KVH = 16, 4
B_G = B_HEADS // B_KVH
C_HEADS, C_KVH, C_HD = 64, 8, 64
C_G = C_HEADS // C_KVH
WINDOW = 128
M_HEADS = 4
NUM_BUCKETS = 32
REL_MAX_DIST = 128
A_Q_W = A_HEADS * 2 * HD
A_KV_W = A_KVH * 2 * HD
B_Q_W = B_HEADS * HD
B_KV_W = B_KVH * HD
EVEN_MAIN = A_Q_W + 2 * A_KV_W + B_Q_W + 2 * B_KV_W
C_Q_W = C_HEADS * C_HD
C_KV_W = C_KVH * C_HD
ODD_IN = C_Q_W + 2 * C_KV_W
M_W = M_HEADS * HD
NEG = -1e30
M_INIT = -1e38
EPS = 1e-6
SCALE = HD ** -0.5
C_SCALE = C_HD ** -0.5
LOG2E = 1.0 / math.log(2.0)

LANES = 128
VMEM_LIMIT = 60 * 1024 * 1024
TQ_A, TK_A = 256, 256
PAGES_PER_STEP = 8


def _cparams(n_axes):
    return pltpu.CompilerParams(dimension_semantics=("arbitrary",) * n_axes,
                                vmem_limit_bytes=VMEM_LIMIT)


def _dot(a, b):
    return jnp.dot(a, b, preferred_element_type=F32)


def _dot_nt(a, b):
    return lax.dot_general(a, b, (((1,), (1,)), ((), ())), preferred_element_type=F32)


def _rep(x, width):
    n = width // LANES
    return x if n == 1 else jnp.concatenate([x] * n, axis=1)


def _rmsnorm_kernel(x_ref, g_ref, o_ref):
    x = x_ref[...]
    ms = jnp.mean(x * x, axis=-1, keepdims=True)
    o_ref[...] = (x * lax.rsqrt(ms + EPS) * g_ref[...]).astype(o_ref.dtype)


def _rmsnorm(x, g):
    m, d = x.shape
    bm = math.gcd(m, 256)
    return pl.pallas_call(
        _rmsnorm_kernel,
        grid=(m // bm,),
        in_specs=[pl.BlockSpec((bm, d), lambda i: (i, 0)),
                  pl.BlockSpec((1, d), lambda i: (0, 0))],
        out_specs=pl.BlockSpec((bm, d), lambda i: (i, 0)),
        out_shape=jax.ShapeDtypeStruct((m, d), _CD),
        compiler_params=_cparams(1),
        name="rmsnorm",
    )(x, g.reshape(1, d).astype(F32))


def _group_rms(x, gs):
    x2 = x * x
    if gs == LANES:
        return lax.rsqrt(jnp.mean(x2, axis=-1, keepdims=True) + EPS)
    lo = lax.broadcasted_iota(jnp.int32, x.shape, 1) < gs
    s_lo = jnp.sum(jnp.where(lo, x2, 0.0), axis=-1, keepdims=True)
    s_hi = jnp.sum(jnp.where(lo, 0.0, x2), axis=-1, keepdims=True)
    return jnp.where(lo, lax.rsqrt(s_lo / gs + EPS), lax.rsqrt(s_hi / gs + EPS))


def _epilogue(accs, vec_ref, epi, gs):
    if epi == "none":
        return accs[0]
    if epi == "silu_mul":
        g, u = accs
        return g * (1.0 / (1.0 + jnp.exp(-g))) * u
    y = accs[0]
    if epi == "logsig":
        z = y + vec_ref[2:3, :]
        ls = jnp.minimum(z, 0.0) - jnp.log(1.0 + jnp.exp(-jnp.abs(z)))
        return jnp.where(vec_ref[1:2, :] > 0, ls, 0.0)
    assert epi == "norm"
    cols = []
    for c in range(y.shape[1] // LANES):
        sl = slice(c * LANES, (c + 1) * LANES)
        x = y[:, sl]
        xn = x * _group_rms(x, gs) * vec_ref[0:1, sl]
        cols.append(jnp.where(vec_ref[1:2, sl] > 0, xn, x))
    return cols[0] if len(cols) == 1 else jnp.concatenate(cols, axis=1)


def _mm_kernel(*refs, n_w, cast_w, has_s, epi, has_res, n_out, gs, i_axis, cast_rows, k_perm):
    it = iter(refs)
    xp = next(it)
    xs = next(it) if has_s else None
    ws = [next(it) for _ in range(n_w)]
    vec = next(it) if epi in ("norm", "logsig") else None
    rp = next(it) if has_res else None
    rs = next(it) if (has_res and has_s) else None
    outs_p = [next(it) for _ in range(n_out)]
    outs_s = [next(it) for _ in range(n_out)] if has_s else []
    wbs = [next(it) for _ in range(n_w)] if cast_w else ws
    i = pl.program_id(i_axis)

    if cast_w:
        @pl.when(i == 0)
        def _():
            for w, wb in zip(ws, wbs):
                for cb in range(w.shape[0] // cast_rows):
                    src = k_perm[cb] if k_perm is not None else cb
                    wb[cb * cast_rows:(cb + 1) * cast_rows, :] = (
                        w[src * cast_rows:(src + 1) * cast_rows, :].astype(_CD))

    def emit(x_ref, r_ref, o_refs):
        y = _epilogue([_dot(x_ref[...], wb[...]) for wb in wbs], vec, epi, gs)
        if r_ref is not None:
            y = y + r_ref[...]
        for o in o_refs:
            o[...] = y.astype(o.dtype)

    emit(xp, rp, outs_p)
    if has_s:
        @pl.when(i == 0)
        def _():
            emit(xs, rs, outs_s)


def _cast_rows(k):
    for r in (512, 256, 128, 64, 32, 16):
        if k % r == 0:
            return r
    raise ValueError(k)


def _mm(xp, xs, ws, layer, n_cols, *, bm, bn, epi="none", vec=None, res=None,
        out_dtypes=(F32,), gs=LANES, k_perm=None, order="ws"):
    mp, k = xp.shape
    has_s = xs is not None
    cast_w = order == "ws"
    assert cast_w or (ws[0].dtype == _CD and not has_s)
    bm = min(bm, mp)
    bn = min(bn, n_cols)
    assert mp % bm == 0 and n_cols % bn == 0
    ni, nj = mp // bm, n_cols // bn
    if order == "ws":
        grid, i_axis = (nj, ni), 1
        ij = lambda a, b: (b, a)
    else:
        grid, i_axis = (ni, nj), 0
        ij = lambda a, b: (a, b)
    cast_rows = 512 if k_perm is not None else _cast_rows(k)

    def xmap(a, b):
        return (ij(a, b)[0], 0)

    def wmap(a, b):
        return (layer, 0, ij(a, b)[1])

    def omap(a, b):
        return ij(a, b)

    def smap(a, b):
        return (0, ij(a, b)[1])

    in_specs = [pl.BlockSpec((bm, k), xmap)]
    args = [xp]
    if has_s:
        ms = xs.shape[0]
        in_specs.append(pl.BlockSpec((ms, k), lambda a, b: (0, 0)))
        args.append(xs)
    for w in ws:
        in_specs.append(pl.BlockSpec((None, k, bn), wmap))
        args.append(w)
    if vec is not None:
        in_specs.append(pl.BlockSpec((8, bn), smap))
        args.append(vec)
    has_res = res is not None
    if has_res:
        in_specs.append(pl.BlockSpec((bm, bn), omap))
        args.append(res[0])
        if has_s:
            in_specs.append(pl.BlockSpec((ms, bn), smap))
            args.append(res[1])
    out_specs, out_shape = [], []
    for dt in out_dtypes:
        out_specs.append(pl.BlockSpec((bm, bn), omap))
        out_shape.append(jax.ShapeDtypeStruct((mp, n_cols), dt))
    if has_s:
        for dt in out_dtypes:
            out_specs.append(pl.BlockSpec((ms, bn), smap))
            out_shape.append(jax.ShapeDtypeStruct((ms, n_cols), dt))
    scratch = [pltpu.VMEM((k, bn), _CD) for _ in ws] if cast_w else []
    kern = functools.partial(
        _mm_kernel, n_w=len(ws), cast_w=cast_w, has_s=has_s, epi=epi, has_res=has_res,
        n_out=len(out_dtypes), gs=gs, i_axis=i_axis, cast_rows=cast_rows, k_perm=k_perm)
    return pl.pallas_call(
        kern, grid=grid, in_specs=in_specs, out_specs=out_specs, out_shape=out_shape,
        scratch_shapes=scratch, compiler_params=_cparams(2), name="proj_" + epi,
    )(*args)


def _vec(n, gain=None, flag=None, bias=None):
    v = jnp.zeros((8, n), F32)
    if gain is not None:
        v = v.at[0].set(gain.astype(F32))
    if flag is not None:
        v = v.at[1].set(flag.astype(F32))
    if bias is not None:
        v = v.at[2].set(bias.astype(F32))
    return v


def _normed_proj_kernel(x_ref, g_ref, w_ref, vec_ref, o_ref, wb_ref, *, gs):
    @pl.when(pl.program_id(0) == 0)
    def _():
        wb_ref[...] = w_ref[...].astype(_CD)

    x = x_ref[...]
    ms = jnp.mean(x * x, axis=-1, keepdims=True)
    h = (x * lax.rsqrt(ms + EPS) * g_ref[...]).astype(_CD)
    y = _epilogue([_dot(h, wb_ref[...])], vec_ref, "norm", gs)
    o_ref[...] = y.astype(o_ref.dtype)


def _normed_proj(x, g, w, layer, vec, *, bm):
    m, d = x.shape
    n = w.shape[2]
    bm = min(bm, m)
    return pl.pallas_call(
        functools.partial(_normed_proj_kernel, gs=LANES),
        grid=(m // bm,),
        in_specs=[pl.BlockSpec((bm, d), lambda i: (i, 0)),
                  pl.BlockSpec((1, d), lambda i: (0, 0)),
                  pl.BlockSpec((None, d, n), lambda i: (layer, 0, 0)),
                  pl.BlockSpec((8, n), lambda i: (0, 0))],
        out_specs=pl.BlockSpec((bm, n), lambda i: (i, 0)),
        out_shape=jax.ShapeDtypeStruct((m, n), _CD),
        scratch_shapes=[pltpu.VMEM((d, n), _CD)],
        compiler_params=_cparams(1),
        name="normed_proj",
    )(x, g.reshape(1, d).astype(F32), w, vec)


def _proj_res_norm_kernel(a_ref, w_ref, r_ref, g_ref, x_ref, h_ref, wb_ref):
    @pl.when(pl.program_id(0) == 0)
    def _():
        wb_ref[...] = w_ref[...].astype(_CD)

    x = r_ref[...] + _dot(a_ref[...], wb_ref[...])
    x_ref[...] = x
    ms = jnp.mean(x * x, axis=-1, keepdims=True)
    h_ref[...] = (x * lax.rsqrt(ms + EPS) * g_ref[...]).astype(h_ref.dtype)


def _proj_res_norm(a, w, layer, res, g, *, bm):
    m, k = a.shape
    d = w.shape[2]
    bm = min(bm, m)
    return pl.pallas_call(
        _proj_res_norm_kernel,
        grid=(m // bm,),
        in_specs=[pl.BlockSpec((bm, k), lambda i: (i, 0)),
                  pl.BlockSpec((None, k, d), lambda i: (layer, 0, 0)),
                  pl.BlockSpec((bm, d), lambda i: (i, 0)),
                  pl.BlockSpec((1, d), lambda i: (0, 0))],
        out_specs=[pl.BlockSpec((bm, d), lambda i: (i, 0)), pl.BlockSpec((bm, d), lambda i: (i, 0))],
        out_shape=[jax.ShapeDtypeStruct((m, d), F32), jax.ShapeDtypeStruct((m, d), _CD)],
        scratch_shapes=[pltpu.VMEM((k, d), _CD)],
        compiler_params=_cparams(1),
        name="proj_res_norm",
    )(a, w, res, g.reshape(1, d).astype(F32))


def _rel_bucket(n):
    max_exact = NUM_BUCKETS // 2
    nf = jnp.maximum(n, 1).astype(F32)
    large = max_exact + (jnp.log(nf / max_exact) / math.log(REL_MAX_DIST / max_exact)
                         * (NUM_BUCKETS - max_exact)).astype(jnp.int32)
    large = jnp.minimum(large, NUM_BUCKETS - 1)
    return jnp.where(n < max_exact, n, large)


def _rel_bias(table, n):
    bucket = _rel_bucket(jnp.maximum(n, 0))
    onehot = (bucket[..., None] == jnp.arange(NUM_BUCKETS)).astype(F32)
    out = jnp.tensordot(onehot, table.astype(F32), axes=1, precision=lax.Precision.HIGHEST)
    return jnp.moveaxis(out, -1, 0)


def _cumsum_kernel(x_ref, o_ref, *, blk):
    t = x_ref.shape[0]
    r = lax.broadcasted_iota(jnp.int32, (blk, blk), 0)
    c = lax.broadcasted_iota(jnp.int32, (blk, blk), 1)
    tri = (c <= r).astype(F32)

    def body(b, carry):
        rows = pl.ds(pl.multiple_of(b * blk, blk), blk)
        y = jnp.dot(tri, x_ref[rows, :], preferred_element_type=F32,
                    precision=lax.Precision.HIGHEST) + carry
        o_ref[rows, :] = y
        return y[blk - 1:blk, :]

    lax.fori_loop(0, t // blk, body, jnp.zeros((1, LANES), F32))


def _cumsum_time(x):
    b, t, w = x.shape
    blk = math.gcd(t, 256)
    return pl.pallas_call(
        functools.partial(_cumsum_kernel, blk=blk),
        grid=(b,),
        in_specs=[pl.BlockSpec((None, t, w), lambda n: (n, 0, 0))],
        out_specs=pl.BlockSpec((None, t, w), lambda n: (n, 0, 0)),
        out_shape=jax.ShapeDtypeStruct((b, t, w), F32),
        compiler_params=_cparams(1),
        name="logf_cumsum",
    )(x)


def _diff_lambda(lam_ref, lam_init):
    la = lam_ref[...]
    e1 = jnp.exp(jnp.sum(la[0:1, :] * la[1:2, :], axis=-1, keepdims=True))
    e2 = jnp.exp(jnp.sum(la[2:3, :] * la[3:4, :], axis=-1, keepdims=True))
    return e1 - e2 + lam_init


def _subln(o, sub_ref, lam_init):
    ms = jnp.mean(o * o, axis=-1, keepdims=True)
    return o * lax.rsqrt(ms + EPS) * sub_ref[...] * (1.0 - lam_init)


def _even_prompt_kernel(aq_ref, ak_ref, av_ref, bq_ref, bk_ref, bv_ref, cq_ref, ck_ref,
                        abias_ref, fmask_ref, lam_ref, sub_ref, o_ref,
                        m_ref, l_ref, acca_ref, accb_ref, *, lam_init):
    tq, tk = TQ_A, TK_A
    qi = pl.program_id(2)
    jd = (qi * tq) // tk
    m_ref[...] = jnp.full(m_ref.shape, M_INIT, F32)
    l_ref[...] = jnp.zeros(l_ref.shape, F32)
    acca_ref[...] = jnp.zeros(acca_ref.shape, F32)
    accb_ref[...] = jnp.zeros(accb_ref.shape, F32)

    def update(s, v, acc_ref, sidx, aidx):
        m_prev = m_ref[sidx]
        m_new = jnp.maximum(m_prev, jnp.max(s, axis=1, keepdims=True))
        p = jnp.exp2(s - _rep(m_new, s.shape[1]))
        alpha = jnp.exp2(m_prev - m_new)
        l_ref[sidx] = alpha * l_ref[sidx] + jnp.sum(p, axis=1, keepdims=True)
        m_ref[sidx] = m_new
        acc_ref[aidx] = acc_ref[aidx] * _rep(alpha, v.shape[1]) + _dot(p.astype(_CD), v)

    def step(j, diag):
        ks = pl.ds(pl.multiple_of(j * tk, tk), tk)
        ka = ak_ref[ks, :]
        va = av_ref[ks, :]
        off = jnp.minimum((qi * tq - j * tk) // LANES, 3)
        for g in range(A_G):
            bias = abias_ref[g, off]
            for c in range(2):
                q = aq_ref[:, (g * 2 + c) * HD:(g * 2 + c + 1) * HD]
                s = _dot_nt(q, ka[:, c * HD:(c + 1) * HD]) * (SCALE * LOG2E) + bias
                update(s, va, acca_ref, g * 2 + c, g * 2 + c)
        kb = bk_ref[ks, :]
        vb = bv_ref[ks, :]
        for g in range(B_G):
            q = bq_ref[:, g * HD:(g + 1) * HD]
            s = _dot_nt(q, kb) * (SCALE * LOG2E) + (cq_ref[:, g:g + 1] - ck_ref[g, pl.ds(j, 1), :])
            if diag:
                s = s + fmask_ref[off]
            update(s, vb, accb_ref, 4 + g, g)

    def body(j, carry):
        step(j, False)
        return carry

    lax.fori_loop(0, jd, body, 0)
    step(jd, True)

    lam = _diff_lambda(lam_ref, lam_init)
    for g in range(A_G):
        o1 = acca_ref[2 * g] / _rep(l_ref[2 * g], 2 * HD)
        o2 = acca_ref[2 * g + 1] / _rep(l_ref[2 * g + 1], 2 * HD)
        o = _subln(o1 - lam * o2, sub_ref, lam_init)
        o_ref[:, g * 2 * HD:(g + 1) * 2 * HD] = o.astype(o_ref.dtype)
    for g in range(B_G):
        o = accb_ref[g] / l_ref[4 + g]
        o_ref[:, A_G * 2 * HD + g * HD:A_G * 2 * HD + (g + 1) * HD] = o.astype(o_ref.dtype)


def _even_prompt_attn(pb, cq, ck, abias, fmask, lam_p, subln, lam_init):
    b, t, _ = pb.shape
    tq, tk = TQ_A, TK_A
    nq, nkb = t // tq, t // tk
    o_w = A_G * 2 * HD + B_G * HD
    ak0 = A_Q_W // (2 * HD)
    av0 = (A_Q_W + A_KV_W) // (2 * HD)
    bq0 = (A_Q_W + 2 * A_KV_W) // (B_G * HD)
    bk0 = (A_Q_W + 2 * A_KV_W + B_Q_W) // HD
    bv0 = bk0 + B_KVH
    in_specs = [
        pl.BlockSpec((None, tq, A_G * 2 * HD), lambda n, h, q: (n, q, h)),
        pl.BlockSpec((None, t, 2 * HD), lambda n, h, q: (n, 0, ak0 + h)),
        pl.BlockSpec((None, t, 2 * HD), lambda n, h, q: (n, 0, av0 + h)),
        pl.BlockSpec((None, tq, B_G * HD), lambda n, h, q: (n, q, bq0 + h)),
        pl.BlockSpec((None, t, HD), lambda n, h, q: (n, 0, bk0 + h)),
        pl.BlockSpec((None, t, HD), lambda n, h, q: (n, 0, bv0 + h)),
        pl.BlockSpec((None, None, tq, B_G), lambda n, h, q: (n, h, q, 0)),
        pl.BlockSpec((None, None, B_G, nkb, tk), lambda n, h, q: (n, h, 0, 0, 0)),
        pl.BlockSpec((A_G, 4, tq, tk), lambda n, h, q: (h, 0, 0, 0)),
        pl.BlockSpec((2, tq, tk), lambda n, h, q: (0, 0, 0)),
        pl.BlockSpec((4, HD), lambda n, h, q: (0, 0)),
        pl.BlockSpec((1, 2 * HD), lambda n, h, q: (0, 0)),
    ]
    return pl.pallas_call(
        functools.partial(_even_prompt_kernel, lam_init=lam_init),
        grid=(b, A_KVH, nq),
        in_specs=in_specs,
        out_specs=pl.BlockSpec((None, tq, o_w), lambda n, h, q: (n, q, h)),
        out_shape=jax.ShapeDtypeStruct((b, t, A_KVH * o_w), _CD),
        scratch_shapes=[pltpu.VMEM((8, tq, LANES), F32), pltpu.VMEM((8, tq, LANES), F32),
                        pltpu.VMEM((4, tq, 2 * HD), F32), pltpu.VMEM((4, tq, HD), F32)],
        compiler_params=_cparams(3),
        name="even_prompt_attn",
    )(pb, pb, pb, pb, pb, pb, cq, ck, abias, fmask, lam_p, subln)


def _even_decode_kernel(pt_ref, qa_ref, qb_ref, kan_ref, van_ref, kbn_ref, vbn_ref, lfn_ref,
                        abias_ref, bmask_ref, later_ref, aself_ref, lam_ref, sub_ref, *rest,
                        n_chunks, lam_init):
    del pt_ref
    npg = PAGES_PER_STEP
    pages = rest[:5 * npg]
    o_ref = rest[5 * npg]
    m_ref, l_ref, acca_ref, accb_ref, run_ref = rest[5 * npg + 1:]
    c = pl.program_id(1)

    @pl.when(c == 0)
    def _():
        sa = jnp.sum(qa_ref[...].astype(F32) * kan_ref[...].astype(F32), axis=1, keepdims=True)
        sb = jnp.sum(qb_ref[...].astype(F32) * kbn_ref[...].astype(F32), axis=1, keepdims=True)
        m_ref[0] = sa * SCALE + aself_ref[...]
        m_ref[1] = jnp.broadcast_to(sb * SCALE, (16, LANES))
        l_ref[...] = jnp.ones(l_ref.shape, F32)
        acca_ref[...] = van_ref[...].astype(F32)
        accb_ref[...] = vbn_ref[...].astype(F32)
        run_ref[...] = lfn_ref[...]

    page = pages[0].shape[0]
    cols = page * A_KVH
    tdims = (((0,), (0,)), ((), ()))

    def update(idx, s, v, acc_ref):
        m_prev = m_ref[idx]
        m_new = jnp.maximum(m_prev, jnp.max(s, axis=1, keepdims=True))
        pr = jnp.exp(s - _rep(m_new, s.shape[1]))
        alpha = jnp.exp(m_prev - m_new)
        l_ref[idx] = alpha * l_ref[idx] + jnp.sum(pr, axis=1, keepdims=True)
        m_ref[idx] = m_new
        acc_ref[...] = acc_ref[...] * _rep(alpha, v.shape[1]) + _dot(pr.astype(_CD), v)

    for p in range(npg):
        ka_ref, va_ref, kb_ref, vb_ref, lf_ref = pages[5 * p:5 * p + 5]
        lf = lf_ref[...]
        hi = lf.astype(_CD)
        r1 = lf - hi.astype(F32)
        mid = r1.astype(_CD)
        lo = (r1 - mid.astype(F32)).astype(_CD)
        sums = lax.dot_general(jnp.concatenate([hi, mid, lo], axis=1), later_ref[...], tdims,
                               preferred_element_type=F32)
        sums = sums[0:16] + sums[16:32] + sums[32:48]
        run = run_ref[...]
        fbias = _rep(run, cols) + sums[:, :cols] + bmask_ref[...]
        run_ref[...] = run + sums[:, cols:]
        xa = ka_ref[...].reshape(cols, 2 * HD).astype(_CD)
        sa = _dot_nt(qa_ref[...], xa) * SCALE + abias_ref[p]
        update(0, sa, va_ref[...].reshape(cols, 2 * HD).astype(_CD), acca_ref)
        xb = kb_ref[...].reshape(cols, HD).astype(_CD)
        sb = _dot_nt(qb_ref[...], xb) * SCALE + fbias
        update(1, sb, vb_ref[...].reshape(cols, HD).astype(_CD), accb_ref)

    @pl.when(c == n_chunks - 1)
    def _():
        lam = _diff_lambda(lam_ref, lam_init)
        o_w = A_G * 2 * HD + B_G * HD
        la = l_ref[0]
        lb = l_ref[1]
        for h in range(A_KVH):
            for g in range(A_G):
                r1 = h * 4 + g * 2
                o1 = acca_ref[r1:r1 + 1, :] / _rep(la[r1:r1 + 1, :], 2 * HD)
                o2 = acca_ref[r1 + 1:r1 + 2, :] / _rep(la[r1 + 1:r1 + 2, :], 2 * HD)
                o = _subln(o1 - lam * o2, sub_ref, lam_init)
                o_ref[0:1, h * o_w + g * 2 * HD:h * o_w + (g + 1) * 2 * HD] = o
            for g in range(B_G):
                rr = h * 4 + g
                o = accb_ref[rr:rr + 1, :] / lb[rr:rr + 1, :]
                o_ref[0:1, h * o_w + A_G * 2 * HD + g * HD:h * o_w + A_G * 2 * HD + (g + 1) * HD] = o


def _even_decode_attn(e, page_table, caches, qa16, qb16, new_kv, lf_new, abias, bmask, later,
                      aself, lam_p, subln, lam_init):
    cache_a_k, cache_a_v, cache_b_k, cache_b_v, cache_b_logf = caches
    db, n_pages = page_table.shape
    page = cache_a_k.shape[2]
    npg = PAGES_PER_STEP
    assert n_pages % npg == 0
    n_chunks = n_pages // npg
    kan, van, kbn, vbn = new_kv
    cols = page * A_KVH

    def fixed(shape):
        nd = len(shape)
        return pl.BlockSpec((None,) + shape, lambda b, c, pt: (b,) + (0,) * nd)

    def const(shape):
        nd = len(shape)
        return pl.BlockSpec(shape, lambda b, c, pt: (0,) * nd)

    in_specs = [
        fixed((16, 2 * HD)), fixed((16, HD)),
        fixed((16, 2 * HD)), fixed((16, 2 * HD)), fixed((16, HD)), fixed((16, HD)),
        fixed((16, LANES)),
        pl.BlockSpec((npg, 16, cols), lambda b, c, pt: (c, 0, 0)),
        const((16, cols)), const((page, cols + LANES)),
        const((16, LANES)), const((4, HD)), const((1, 2 * HD)),
    ]
    args = [qa16, qb16, kan, van, kbn, vbn, lf_new, abias, bmask, later, aself, lam_p, subln]
    for p in range(npg):
        def pmap(b, c, pt, p=p):
            return (e, pt[b, n_pages - 1 - (c * npg + p)], 0, 0, 0)

        def fmap(b, c, pt, p=p):
            return (e, pt[b, n_pages - 1 - (c * npg + p)], 0, 0)

        in_specs += [
            pl.BlockSpec((None, None, page, A_KVH, 2 * HD), pmap),
            pl.BlockSpec((None, None, page, A_KVH, 2 * HD), pmap),
            pl.BlockSpec((None, None, page, B_KVH, HD), pmap),
            pl.BlockSpec((None, None, page, B_KVH, HD), pmap),
            pl.BlockSpec((None, None, page, B_HEADS), fmap),
        ]
        args += [cache_a_k, cache_a_v, cache_b_k, cache_b_v, cache_b_logf]
    o_w = A_KVH * (A_G * 2 * HD + B_G * HD)
    out = pl.pallas_call(
        functools.partial(_even_decode_kernel, n_chunks=n_chunks, lam_init=lam_init),
        grid_spec=pltpu.PrefetchScalarGridSpec(
            num_scalar_prefetch=1,
            grid=(db, n_chunks),
            in_specs=in_specs,
            out_specs=pl.BlockSpec((None, 1, o_w), lambda b, c, pt: (b, 0, 0)),
            scratch_shapes=[pltpu.VMEM((2, 16, LANES), F32), pltpu.VMEM((2, 16, LANES), F32),
                            pltpu.VMEM((16, 2 * HD), F32), pltpu.VMEM((16, HD), F32),
                            pltpu.VMEM((16, LANES), F32)],
        ),
        out_shape=jax.ShapeDtypeStruct((db, 1, o_w), F32),
        compiler_params=_cparams(2),
        name="even_decode_attn",
    )(page_table, *args)
    return out.reshape(db, o_w)


def _swa_prompt_kernel(sink_ref, q_ref, kp_ref, kc_ref, vp_ref, vc_ref, bias_ref, o_ref):
    kp = pl.program_id(2)
    w = WINDOW
    k_cat = jnp.concatenate([kp_ref[...], kc_ref[...]], axis=0)
    v_cat = jnp.concatenate([vp_ref[...], vc_ref[...]], axis=0)
    outs = []
    for kv in range(2):
        k_h = k_cat[:, kv * C_HD:(kv + 1) * C_HD]
        v_h = v_cat[:, kv * C_HD:(kv + 1) * C_HD]
        for g in range(C_G):
            hh = kv * C_G + g
            q = q_ref[:, hh * C_HD:(hh + 1) * C_HD]
            s = _dot_nt(q, k_h) + bias_ref[hh]
            sink = sink_ref[kp * 2 * C_G + hh]
            m = jnp.maximum(jnp.max(jnp.maximum(s[:, :w], s[:, w:]), axis=1, keepdims=True), sink)
            e = jnp.exp(s - m)
            den = jnp.sum(e[:, :w] + e[:, w:], axis=1, keepdims=True) + jnp.exp(sink - m)
            outs.append(_dot((e * (1.0 / den)).astype(_CD), v_h))
    o_ref[...] = jnp.concatenate(outs, axis=1).astype(o_ref.dtype)


def _swa_prompt_attn(pb, bias, sinks):
    b, t, _ = pb.shape
    w = WINDOW
    nq = t // w
    qw = 2 * C_G * C_HD
    k0 = C_Q_W // LANES
    v0 = (C_Q_W + C_KV_W) // LANES
    prev = lambda q: jnp.maximum(q - 1, 0)
    in_specs = [
        pl.BlockSpec(memory_space=pltpu.SMEM),
        pl.BlockSpec((None, w, qw), lambda n, q, k: (n, q, k)),
        pl.BlockSpec((None, w, LANES), lambda n, q, k: (n, prev(q), k0 + k)),
        pl.BlockSpec((None, w, LANES), lambda n, q, k: (n, q, k0 + k)),
        pl.BlockSpec((None, w, LANES), lambda n, q, k: (n, prev(q), v0 + k)),
        pl.BlockSpec((None, w, LANES), lambda n, q, k: (n, q, v0 + k)),
        pl.BlockSpec((None, 2 * C_G, w, 2 * w), lambda n, q, k: (jnp.minimum(q, 1), k, 0, 0)),
    ]
    return pl.pallas_call(
        _swa_prompt_kernel,
        grid=(b, nq, C_KVH // 2),
        in_specs=in_specs,
        out_specs=pl.BlockSpec((None, w, qw), lambda n, q, k: (n, q, k)),
        out_shape=jax.ShapeDtypeStruct((b, t, C_Q_W), _CD),
        compiler_params=_cparams(3),
        name="swa_prompt_attn",
    )(sinks.astype(F32), pb, pb, pb, pb, pb, bias)


def _swa_decode_kernel(q_ref, kbuf_ref, vbuf_ref, kn_ref, vn_ref, bias_ref, bself_ref, sink_ref,
                       o_ref):
    cols = kbuf_ref.shape[0] * C_KVH
    q = q_ref[...]
    s = _dot_nt(q, kbuf_ref[...].reshape(cols, C_HD).astype(_CD)) + bias_ref[...]
    s_new = (jnp.sum(q.astype(F32) * kn_ref[...].astype(F32), axis=1, keepdims=True)
             + bself_ref[...])
    sink = sink_ref[...]
    m = jnp.maximum(jnp.maximum(jnp.max(s, axis=1, keepdims=True), s_new), sink)
    e = jnp.exp(s - m)
    e_new = jnp.exp(s_new - m)
    den = jnp.sum(e, axis=1, keepdims=True) + e_new + jnp.exp(sink - m)
    inv = 1.0 / den
    o = (e_new * inv).astype(_CD).astype(F32) * vn_ref[...].astype(F32)
    o_ref[...] = o + _dot((e * inv).astype(_CD), vbuf_ref[...].reshape(cols, C_HD).astype(_CD))


def _swa_decode_attn(o_idx, q64, cache_c_k, cache_c_v, kn, vn, bias, bself, sinks):
    db = q64.shape[0]
    cbuf = cache_c_k.shape[2]
    in_specs = [
        pl.BlockSpec((None, C_HEADS, C_HD), lambda b: (b, 0, 0)),
        pl.BlockSpec((None, None, cbuf, C_KVH, C_HD), lambda b: (o_idx, b, 0, 0, 0)),
        pl.BlockSpec((None, None, cbuf, C_KVH, C_HD), lambda b: (o_idx, b, 0, 0, 0)),
        pl.BlockSpec((None, C_HEADS, C_HD), lambda b: (b, 0, 0)),
        pl.BlockSpec((None, C_HEADS, C_HD), lambda b: (b, 0, 0)),
        pl.BlockSpec((C_HEADS, cbuf * C_KVH), lambda b: (0, 0)),
        pl.BlockSpec((C_HEADS, 1), lambda b: (0, 0)),
        pl.BlockSpec((C_HEADS, 1), lambda b: (0, 0)),
    ]
    out = pl.pallas_call(
        _swa_decode_kernel,
        grid=(db,),
        in_specs=in_specs,
        out_specs=pl.BlockSpec((None, C_HEADS, C_HD), lambda b: (b, 0, 0)),
        out_shape=jax.ShapeDtypeStruct((db, C_HEADS, C_HD), F32),
        compiler_params=_cparams(1),
        name="swa_decode_attn",
    )(q64, cache_c_k, cache_c_v, kn, vn, bias, bself, sinks)
    return out.reshape(db, C_Q_W)


def _cross_prompt_kernel(q_ref, k_ref, v_ref, o_ref):
    for h in range(M_HEADS):
        sl = slice(h * HD, (h + 1) * HD)
        s = _dot_nt(q_ref[:, sl], k_ref[:, sl]) * SCALE
        m = jnp.max(s, axis=1, keepdims=True)
        e = jnp.exp(s - m)
        p = e * (1.0 / jnp.sum(e, axis=1, keepdims=True))
        o_ref[:, sl] = _dot(p.astype(_CD), v_ref[:, sl]).astype(o_ref.dtype)


def _cross_prompt_attn(q, mkv):
    b, t, _ = q.shape
    nm = mkv.shape[1]
    tq = math.gcd(t, 512)
    return pl.pallas_call(
        _cross_prompt_kernel,
        grid=(b, t // tq),
        in_specs=[pl.BlockSpec((None, tq, M_W), lambda n, i: (n, i, 0)),
                  pl.BlockSpec((None, nm, M_W), lambda n, i: (n, 0, 0)),
                  pl.BlockSpec((None, nm, M_W), lambda n, i: (n, 0, 1))],
        out_specs=pl.BlockSpec((None, tq, M_W), lambda n, i: (n, i, 0)),
        out_shape=jax.ShapeDtypeStruct((b, t, M_W), _CD),
        compiler_params=_cparams(2),
        name="cross_prompt_attn",
    )(q, mkv, mkv)


def _cross_decode_kernel(q_ref, k_ref, v_ref, mask_ref, o_ref):
    cols = k_ref.shape[0] * M_HEADS
    s = _dot_nt(q_ref[...], k_ref[...].reshape(cols, HD).astype(_CD)) * SCALE + mask_ref[...]
    m = jnp.max(s, axis=1, keepdims=True)
    e = jnp.exp(s - m)
    p = (e * (1.0 / jnp.sum(e, axis=1, keepdims=True))).astype(_CD)
    o_ref[...] = _dot(p, v_ref[...].reshape(cols, HD).astype(_CD))


def _cross_decode_attn(layer, q8, cache_m_k, cache_m_v):
    db = q8.shape[0]
    nm = cache_m_k.shape[2]
    cols = nm * M_HEADS
    mask = jnp.where((jnp.arange(8) % M_HEADS)[:, None] == (jnp.arange(cols) % M_HEADS)[None, :],
                     0.0, NEG).astype(F32)
    out = pl.pallas_call(
        _cross_decode_kernel,
        grid=(db,),
        in_specs=[pl.BlockSpec((None, 8, HD), lambda b: (b, 0, 0)),
                  pl.BlockSpec((None, None, nm, M_HEADS, HD), lambda b: (layer, b, 0, 0, 0)),
                  pl.BlockSpec((None, None, nm, M_HEADS, HD), lambda b: (layer, b, 0, 0, 0)),
                  pl.BlockSpec((8, cols), lambda b: (0, 0))],
        out_specs=pl.BlockSpec((None, 8, HD), lambda b: (b, 0, 0)),
        out_shape=jax.ShapeDtypeStruct((db, 8, HD), F32),
        compiler_params=_cparams(1),
        name="cross_decode_attn",
    )(q8, cache_m_k, cache_m_v, mask)
    return out[:, :M_HEADS, :].reshape(db, M_W)


def kernel(x_prompt, x_sample, mem_prompt, cache_a_k, cache_a_v, cache_b_k, cache_b_v, cache_b_logf, cache_c_k, cache_c_v, cache_m_k, cache_m_v, page_table, rel_table, norm_mix, norm_cross, norm_mem, norm_ffn, w_in_even, b_forget, a_q_norm, a_k_norm, a_lambda, a_subln, b_q_norm, b_k_norm, w_out_even, w_in_odd, c_q_norm, c_k_norm, c_sinks, w_out_odd, w_mq, w_mkv, m_q_norm, m_k_norm, w_mo, w_gate, w_up, w_down):
    b, t, d = x_prompt.shape
    db = x_sample.shape[0]
    assert x_sample.shape[1] == 1
    depth = norm_mix.shape[0]
    n_mem = mem_prompt.shape[1]
    n_pages = page_table.shape[1]
    page = cache_a_k.shape[2]
    past_len = n_pages * page
    cbuf = cache_c_k.shape[2]
    d_ff = w_gate.shape[2]
    mp = b * t
    assert t % TK_A == 0 and t % WINDOW == 0 and cbuf == WINDOW and t >= cbuf

    table_a = rel_table[:, :A_HEADS]
    table_c = rel_table[:, A_HEADS:]

    rr = jnp.arange(TQ_A)[:, None]
    cc = jnp.arange(TK_A)[None, :]
    tiles, masks = [], []
    for u in range(4):
        n = u * LANES + rr - cc
        bt = _rel_bias(table_a, n)
        if u * LANES < TK_A - 1:
            bt = jnp.where((n >= 0)[None], bt, NEG)
            masks.append(jnp.where(n >= 0, 0.0, NEG).astype(F32))
        tiles.append(bt)
    assert 3 * LANES - (TK_A - 1) >= REL_MAX_DIST
    abias_p = jnp.stack(tiles, axis=1) * LOG2E
    fmask_p = jnp.stack(masks, axis=0) * LOG2E

    dist_past = past_len - jnp.arange(past_len)
    row_head = jnp.array([h * A_G + g for h in range(A_KVH) for g in range(A_G) for _ in range(2)])
    ab = _rel_bias(table_a, dist_past)[row_head]
    ab = ab.reshape(16, n_pages, page).transpose(1, 0, 2)[::-1]
    cols = page * A_KVH
    head_mask = jnp.where((jnp.arange(16) // 4)[:, None] == (jnp.arange(cols) % A_KVH)[None, :],
                          0.0, NEG).astype(F32)
    ab = jnp.repeat(ab, A_KVH, axis=2) + head_mask[None]
    later = jnp.concatenate(
        [(jnp.arange(page)[:, None] > (jnp.arange(cols) // A_KVH)[None, :]).astype(_CD),
         jnp.ones((page, LANES), _CD)], axis=1)
    aself = jnp.broadcast_to(_rel_bias(table_a, jnp.zeros((1,), jnp.int32))[row_head], (16, LANES))

    wi = jnp.arange(WINDOW)[:, None]
    wj = jnp.arange(2 * WINDOW)[None, :]
    nband = wi + WINDOW - wj
    cbias_p = jnp.where(((nband >= 0) & (nband <= WINDOW))[None], _rel_bias(table_c, nband), NEG)
    cbias_p = jnp.stack([jnp.where((wj >= WINDOW)[None], cbias_p, NEG), cbias_p])
    dist_buf = cbuf - jnp.arange(cbuf)
    cbias_d = jnp.where((dist_buf <= WINDOW)[None], _rel_bias(table_c, dist_buf), NEG)
    ccols = cbuf * C_KVH
    cbias_d = jnp.repeat(cbias_d, C_KVH, axis=1) + jnp.where(
        (jnp.arange(C_HEADS) // C_G)[:, None] == (jnp.arange(ccols) % C_KVH)[None, :], 0.0, NEG)
    cself = _rel_bias(table_c, jnp.zeros((1,), jnp.int32))

    xp = x_prompt.reshape(mp, d)
    xs = x_sample.reshape(db, d)
    outs = {k: [] for k in ("ak_p", "ak_s", "av_p", "av_s", "bk_p", "bk_s", "bv_p", "bv_s",
                            "lf_p", "lf_s", "ck_p", "ck_s", "cv_p", "cv_s", "mk", "mv")}
    w_down_c = w_down.astype(_CD)
    mem2 = mem_prompt.reshape(b * n_mem, d)

    for l in range(depth):
        hp = _rmsnorm(xp, norm_mix[l])
        hs = _rmsnorm(xs, norm_mix[l])
        if l % 2 == 0:
            e = l // 2
            lam_init = 0.8 - 0.6 * math.exp(-0.3 * l)
            gain = jnp.concatenate([
                jnp.tile(a_q_norm[e].reshape(-1), A_HEADS), jnp.tile(a_k_norm[e].reshape(-1), A_KVH),
                jnp.ones((A_KV_W,), F32), jnp.tile(b_q_norm[e], B_HEADS), jnp.tile(b_k_norm[e], B_KVH),
                jnp.ones((B_KV_W,), F32)])
            flag = jnp.concatenate([jnp.ones((A_Q_W + A_KV_W,)), jnp.zeros((A_KV_W,)),
                                    jnp.ones((B_Q_W + B_KV_W,)), jnp.zeros((B_KV_W,))])
            pf, pc, sf, sc = _mm(hp, hs, [w_in_even], e, EVEN_MAIN, bm=1024, bn=512, epi="norm",
                                 vec=_vec(EVEN_MAIN, gain, flag), out_dtypes=(F32, _CD))
            w_f = jnp.pad(w_in_even[e][:, EVEN_MAIN:], ((0, 0), (0, LANES - B_HEADS)))[None]
            fvec = _vec(LANES, flag=jnp.arange(LANES) < B_HEADS,
                        bias=jnp.pad(b_forget[e], (0, LANES - B_HEADS)))
            lf_p, lf_s = _mm(hp, hs, [w_f], 0, LANES, bm=1024, bn=LANES, epi="logsig", vec=fvec)

            o1, o2, o3, o4 = A_Q_W, A_Q_W + A_KV_W, A_Q_W + 2 * A_KV_W, A_Q_W + 2 * A_KV_W + B_Q_W
            o5 = o4 + B_KV_W
            outs["ak_p"].append(pf[:, o1:o2].reshape(b, t, A_KVH, 2 * HD))
            outs["av_p"].append(pf[:, o2:o3].reshape(b, t, A_KVH, 2 * HD))
            outs["bk_p"].append(pf[:, o4:o5].reshape(b, t, B_KVH, HD))
            outs["bv_p"].append(pf[:, o5:].reshape(b, t, B_KVH, HD))
            outs["lf_p"].append(lf_p[:, :B_HEADS].reshape(b, t, B_HEADS))
            outs["ak_s"].append(sf[:, o1:o2].reshape(db, 1, A_KVH, 2 * HD))
            outs["av_s"].append(sf[:, o2:o3].reshape(db, 1, A_KVH, 2 * HD))
            outs["bk_s"].append(sf[:, o4:o5].reshape(db, 1, B_KVH, HD))
            outs["bv_s"].append(sf[:, o5:].reshape(db, 1, B_KVH, HD))
            outs["lf_s"].append(lf_s[:, :B_HEADS].reshape(db, 1, B_HEADS))

            csum = _cumsum_time(lf_p.reshape(b, t, LANES))[:, :, :B_HEADS] * LOG2E
            cq = csum.reshape(b, t, B_KVH, B_G).transpose(0, 2, 1, 3)
            ck = csum.transpose(0, 2, 1).reshape(b, B_KVH, B_G, t // TK_A, TK_A)
            lam_p = a_lambda[e].astype(F32)
            subln = a_subln[e].reshape(1, 2 * HD).astype(F32)
            op = _even_prompt_attn(pc.reshape(b, t, EVEN_MAIN), cq, ck, abias_p, fmask_p,
                                   lam_p, subln, lam_init).reshape(mp, -1)

            qa16 = sc[:, :A_Q_W].reshape(db, A_KVH, A_G, 2, HD)
            eye2 = jnp.eye(2, dtype=_CD)
            qa16 = (qa16[:, :, :, :, None, :] * eye2[None, None, None, :, :, None]).reshape(
                db, 16, 2 * HD)
            qb16 = sc[:, o3:o4].reshape(db, 16, HD)
            per_row = lambda x, w: jnp.repeat(x.reshape(db, A_KVH, w), 4, axis=1)
            new_kv = (per_row(sc[:, o1:o2], 2 * HD), per_row(sc[:, o2:o3], 2 * HD),
                      per_row(sc[:, o4:o5], HD), per_row(sc[:, o5:], HD))
            lf_new = jnp.broadcast_to(lf_s[:, :B_HEADS, None], (db, B_HEADS, LANES))
            os_ = _even_decode_attn(e, page_table,
                                    (cache_a_k, cache_a_v, cache_b_k, cache_b_v, cache_b_logf),
                                    qa16, qb16, new_kv, lf_new, ab, head_mask, later, aself,
                                    lam_p, subln, lam_init)
            k_perm = [(kb % 2) * A_KVH + kb // 2 for kb in range(2 * A_KVH)]
            xp, xs = _mm(op, os_.astype(_CD), [w_out_even], e, d, bm=1024, bn=512,
                         res=(xp, xs), k_perm=k_perm)
        else:
            o_ = l // 2
            gain = jnp.concatenate([jnp.tile(c_q_norm[o_] * C_SCALE, C_HEADS),
                                    jnp.tile(c_k_norm[o_], C_KVH), jnp.ones((C_KV_W,), F32)])
            flag = jnp.concatenate([jnp.ones((C_Q_W + C_KV_W,)), jnp.zeros((C_KV_W,))])
            pf, pc, sf, sc = _mm(hp, hs, [w_in_odd], o_, ODD_IN, bm=1024, bn=512, epi="norm",
                                 vec=_vec(ODD_IN, gain, flag), out_dtypes=(F32, _CD), gs=C_HD)
            kf = pf[:, C_Q_W:C_Q_W + C_KV_W].reshape(b, t, C_KVH, C_HD)
            vf = pf[:, C_Q_W + C_KV_W:].reshape(b, t, C_KVH, C_HD)
            outs["ck_p"].append(kf[:, t - cbuf:])
            outs["cv_p"].append(vf[:, t - cbuf:])
            ks_new = sf[:, C_Q_W:C_Q_W + C_KV_W].reshape(db, 1, C_KVH, C_HD)
            vs_new = sf[:, C_Q_W + C_KV_W:].reshape(db, 1, C_KVH, C_HD)
            outs["ck_s"].append(jnp.concatenate([cache_c_k[o_][:, 1:], ks_new], axis=1))
            outs["cv_s"].append(jnp.concatenate([cache_c_v[o_][:, 1:], vs_new], axis=1))

            op = _swa_prompt_attn(pc.reshape(b, t, ODD_IN), cbias_p, c_sinks[o_]).reshape(mp, C_Q_W)
            q64 = sc[:, :C_Q_W].reshape(db, C_HEADS, C_HD)
            kn = jnp.repeat(sc[:, C_Q_W:C_Q_W + C_KV_W].reshape(db, C_KVH, C_HD), C_G, axis=1)
            vn = jnp.repeat(sc[:, C_Q_W + C_KV_W:].reshape(db, C_KVH, C_HD), C_G, axis=1)
            os_ = _swa_decode_attn(o_, q64, cache_c_k, cache_c_v, kn, vn, cbias_d, cself,
                                   c_sinks[o_].reshape(C_HEADS, 1).astype(F32))
            xp, xs = _mm(op, os_.astype(_CD), [w_out_odd], o_, d, bm=1024, bn=512, res=(xp, xs))

        hm = _rmsnorm(mem2, norm_mem[l])
        mgain = jnp.concatenate([jnp.tile(m_k_norm[l], M_HEADS), jnp.ones((M_W,), F32)])
        mflag = jnp.concatenate([jnp.ones((M_W,)), jnp.zeros((M_W,))])
        mf, mc = _mm(hm, None, [w_mkv], l, 2 * M_W, bm=1024, bn=512, epi="norm",
                     vec=_vec(2 * M_W, mgain, mflag), out_dtypes=(F32, _CD))
        outs["mk"].append(mf[:, :M_W].reshape(b, n_mem, M_HEADS, HD))
        outs["mv"].append(mf[:, M_W:].reshape(b, n_mem, M_HEADS, HD))
        qvec = _vec(M_W, jnp.tile(m_q_norm[l], M_HEADS), jnp.ones((M_W,)))
        qp = _normed_proj(xp, norm_cross[l], w_mq, l, qvec, bm=512)
        qs = _normed_proj(xs, norm_cross[l], w_mq, l, qvec, bm=512)
        op = _cross_prompt_attn(qp.reshape(b, t, M_W), mc.reshape(b, n_mem, 2 * M_W)).reshape(mp, M_W)
        q8 = jnp.pad(qs.reshape(db, M_HEADS, HD), ((0, 0), (0, 8 - M_HEADS), (0, 0)))
        os_ = _cross_decode_attn(l, q8, cache_m_k, cache_m_v)

        xp, hp = _proj_res_norm(op, w_mo, l, xp, norm_ffn[l], bm=256)
        xs, hs = _proj_res_norm(os_.astype(_CD), w_mo, l, xs, norm_ffn[l], bm=256)

        ap, as_ = _mm(hp, hs, [w_gate, w_up], l, d_ff, bm=1024, bn=256, epi="silu_mul",
                      out_dtypes=(_CD,))
        (xp,) = _mm(ap, None, [w_down_c], l, d, bm=512, bn=512, res=(xp,), order="xs")
        (xs,) = _mm(as_, None, [w_down_c], l, d, bm=512, bn=512, res=(xs,), order="xs")

    st = lambda k: jnp.stack(outs[k])
    return (xp.reshape(b, t, d), xs.reshape(db, 1, d),
            st("ak_p"), st("ak_s"), st("av_p"), st("av_s"), st("bk_p"), st("bk_s"),
            st("bv_p"), st("bv_s"), st("lf_p"), st("lf_s"), st("ck_p"), st("ck_s"),
            st("cv_p"), st("cv_s"), st("mk"), st("mv"))
```
